```python
import jax, jax.numpy as jnp
from jax import lax
import numpy as np

D_MODEL = 1024
BATCH = 32
SEQ = 2048
DEPTH = 4
DEC_BATCH = 16
DEC_SEQ = 4096
PAST_LEN = 128

GRID_W = 64
N_MIXERS = 2
POOL_WINDOWS = (2, 4, 8, 16)
N_POOL_GROUPS = 4
POOL_GROUP = D_MODEL // N_POOL_GROUPS
N_HEADS = 16
HEAD_DIM = D_MODEL // N_HEADS
WIN_ROWS = 8
WIN_COLS = 16
Q_BLOCK_COLS = 16
K_BLOCK_COLS = Q_BLOCK_COLS + WIN_COLS
N_COL_BLOCKS = GRID_W // Q_BLOCK_COLS
D_FF = 4 * D_MODEL
RMS_EPS = 1e-6
NEG_INF = -1e30
N_POOL_LAYERS = (DEPTH + 1) // 2
N_ATTN_LAYERS = DEPTH // 2

kernel_name = "hybrid_pool_natten_encoder"


def rms_norm(x, g):
    xf = x.astype(jnp.float32)
    y = xf * lax.rsqrt(jnp.mean(xf * xf, axis=-1, keepdims=True) + RMS_EPS)
    return (y * g.astype(jnp.float32)).astype(x.dtype)


def pool_mixer(h, w_groups, scale):
    b, s, _ = h.shape
    hg = h.reshape(b, s, N_POOL_GROUPS, POOL_GROUP).astype(jnp.float32)
    cs = jnp.concatenate([jnp.zeros((b, 1, N_POOL_GROUPS, POOL_GROUP), jnp.float32),
                          jnp.cumsum(hg, axis=1)], axis=1)
    t = jnp.arange(s)[:, None]
    w = jnp.array(POOL_WINDOWS, dtype=jnp.int32)[None, :]
    lo = jnp.clip(t - w // 2, 0, s - 1)
    hi = jnp.clip(t + w // 2 - 1, 0, s - 1)
    g_idx = jnp.arange(N_POOL_GROUPS)[None, :]
    window_sum = cs[:, hi + 1, g_idx, :] - cs[:, lo, g_idx, :]
    count = (hi - lo + 1).astype(jnp.float32)[None, :, :, None]
    pooled = (window_sum / count - hg).astype(h.dtype)
    mixed = jnp.einsum('bsgc,gcd->bsgd', pooled, w_groups)
    return mixed.reshape(b, s, D_MODEL) * scale


def _column_structure():
    qc = np.arange(GRID_W).reshape(N_COL_BLOCKS, Q_BLOCK_COLS)
    kb_start = np.clip(np.arange(N_COL_BLOCKS) * Q_BLOCK_COLS - WIN_COLS // 2, 0, GRID_W - K_BLOCK_COLS)
    kc = kb_start[:, None] + np.arange(K_BLOCK_COLS)[None, :]
    win_start = np.clip(qc - WIN_COLS // 2, 0, GRID_W - WIN_COLS)
    col_valid = (kc[:, None, :] >= win_start[:, :, None]) & (kc[:, None, :] < win_start[:, :, None] + WIN_COLS)
    dc_idx = np.clip(kc[:, None, :] - qc[:, :, None] + WIN_COLS - 1, 0, 2 * WIN_COLS - 2)
    return kc, col_valid, dc_idx


def neighborhood_attention(h, w_qkv, w_o, rpb):
    b, s, _ = h.shape
    rows = s // GRID_W
    kr = min(WIN_ROWS, rows)
    qkv = h @ w_qkv
    q, k, v = jnp.split(qkv, 3, axis=-1)
    q = q.reshape(b, rows, GRID_W, N_HEADS, HEAD_DIM) * (HEAD_DIM ** -0.5)
    k = k.reshape(b, rows, GRID_W, N_HEADS, HEAD_DIM)
    v = v.reshape(b, rows, GRID_W, N_HEADS, HEAD_DIM)
    kc_np, col_valid_np, dc_idx_np = _column_structure()
    kc = jnp.asarray(kc_np)
    col_valid = jnp.asarray(col_valid_np)[:, :, None, :]
    dc_idx = jnp.asarray(dc_idx_np)[:, :, None, :]

    def row_block(args):
        r, q_row = args
        r0 = jnp.clip(r - kr // 2, 0, rows - kr)
        k_rows = lax.dynamic_slice_in_dim(k, r0, kr, axis=1)
        v_rows = lax.dynamic_slice_in_dim(v, r0, kr, axis=1)
        k_blk = k_rows[:, :, kc]
        v_blk = v_rows[:, :, kc]
        qb = q_row.reshape(b, N_COL_BLOCKS, Q_BLOCK_COLS, N_HEADS, HEAD_DIM)
        sc = jnp.einsum('bjqhd,brjchd->bhjqrc', qb, k_blk,
                        preferred_element_type=jnp.float32)
        dr_idx = (r0 + jnp.arange(kr) - r + WIN_ROWS - 1)[None, None, :, None]
        bias = rpb[:, dr_idx, dc_idx]
        sc = jnp.where(col_valid, sc + bias[None].astype(jnp.float32), NEG_INF)
        p = jax.nn.softmax(sc, axis=(-2, -1)).astype(v.dtype)
        o = jnp.einsum('bhjqrc,brjchd->bjqhd', p, v_blk)
        return o.reshape(b, GRID_W, D_MODEL)

    out = lax.map(row_block, (jnp.arange(rows), jnp.moveaxis(q, 1, 0)))
    out = jnp.moveaxis(out, 0, 1).reshape(b, s, D_MODEL)
    return out @ w_o


def trunk(x, norm_mix, pool_w, pool_scale, w_qkv, rpb, w_o, norm_mlp, w_up, w_down, norm_final):
    for i in range(DEPTH):
        h = rms_norm(x, norm_mix[i])
        j = i // N_MIXERS
        if i % N_MIXERS == 0:
            x = x + pool_mixer(h, pool_w[j], pool_scale[j])
        else:
            x = x + neighborhood_attention(h, w_qkv[j], w_o[j], rpb[j])
        h = rms_norm(x, norm_mlp[i])
        x = x + jnp.square(jax.nn.relu(h @ w_up[i])) @ w_down[i]
    return rms_norm(x, norm_final)


def setup_inputs(seed: int = 0) -> dict:
    key = jax.random.key(seed)
    ks = jax.random.split(key, 12)
    f32 = jnp.float32
    return {
        "x_prompt": jax.random.normal(ks[0], (BATCH, SEQ, D_MODEL), f32),
        "x_sample": jax.random.normal(ks[1], (DEC_BATCH, DEC_SEQ, D_MODEL), f32),
        "norm_mix": 1.0 + 0.02 * jax.random.normal(ks[2], (DEPTH, D_MODEL), f32),
        "pool_w": jax.random.normal(ks[3], (N_POOL_LAYERS, N_POOL_GROUPS, POOL_GROUP, POOL_GROUP), f32) * POOL_GROUP ** -0.5,
        "pool_scale": 1.0 + 0.02 * jax.random.normal(ks[4], (N_POOL_LAYERS, D_MODEL), f32),
        "w_qkv": jax.random.normal(ks[5], (N_ATTN_LAYERS, D_MODEL, 3 * D_MODEL), f32) * D_MODEL ** -0.5,
        "rpb": 0.1 * jax.random.normal(ks[6], (N_ATTN_LAYERS, N_HEADS, 2 * WIN_ROWS - 1, 2 * WIN_COLS - 1), f32),
        "w_o": jax.random.normal(ks[7], (N_ATTN_LAYERS, D_MODEL, D_MODEL), f32) * D_MODEL ** -0.5,
        "norm_mlp": 1.0 + 0.02 * jax.random.normal(ks[8], (DEPTH, D_MODEL), f32),
        "w_up": jax.random.normal(ks[9], (DEPTH, D_MODEL, D_FF), f32) * D_MODEL ** -0.5,
        "w_down": jax.random.normal(ks[10], (DEPTH, D_FF, D_MODEL), f32) * D_FF ** -0.5,
        "norm_final": 1.0 + 0.02 * jax.random.normal(ks[11], (D_MODEL,), f32),
    }


def reference(x_prompt, x_sample, norm_mix, pool_w, pool_scale, w_qkv, rpb, w_o, norm_mlp, w_up, w_down, norm_final):
    y_prompt = trunk(x_prompt, norm_mix, pool_w, pool_scale, w_qkv, rpb, w_o, norm_mlp, w_up, w_down, norm_final)
    y_sample = trunk(x_sample, norm_mix, pool_w, pool_scale, w_qkv, rpb, w_o, norm_mlp, w_up, w_down, norm_final)
    return (y_prompt, y_sample)
```

```python
import functools

import numpy as np
import jax
import jax.numpy as jnp
from jax import lax
from jax.experimental import pallas as pl
from jax.experimental.pallas import tpu as pltpu

D_MODEL = 1024
D_FF = 4 * D_MODEL
GRID_W = 64
POOL_WINDOWS = (2, 4, 8, 16)
N_POOL_GROUPS = len(POOL_WINDOWS)
POOL_GROUP = D_MODEL // N_POOL_GROUPS
POOL_HALO = 8
N_HEADS = 16
HEAD_DIM = D_MODEL // N_HEADS
HEADS_PER_BLOCK = 2
N_HEAD_BLOCKS = N_HEADS // HEADS_PER_BLOCK
LANES = HEADS_PER_BLOCK * HEAD_DIM
WIN_ROWS = 8
WIN_COLS = 16
WIN_KEYS = WIN_ROWS * GRID_W
RMS_EPS = 1e-6
NEG_INF = -1e30

TOKEN_TILE = 512
FF_CHUNK = 1024
VMEM_LIMIT_BYTES = 56 * 1024 * 1024

BF16 = jnp.bfloat16
F32 = jnp.float32


def _rms(x, g):
    ms = jnp.mean(x * x, axis=-1, keepdims=True)
    return x * lax.rsqrt(ms + RMS_EPS) * g


def _params(n_grid):
    return pltpu.CompilerParams(
        dimension_semantics=("arbitrary",) * n_grid,
        vmem_limit_bytes=VMEM_LIMIT_BYTES,
    )


def _const_spec(shape):
    nd = len(shape)
    return pl.BlockSpec(shape, lambda *_: (0,) * nd, pipeline_mode=pl.Buffered(1))


def _pool_kernel(x_ref, xp_ref, xn_ref, g_ref, w_ref, sc_ref, o_ref, ext_ref, *, ts, seq):
    s = pl.program_id(1)
    g = g_ref[...]
    x = x_ref[0]
    h = _rms(x, g)
    hp = jnp.where(s > 0, _rms(xp_ref[0], g), 0.0)
    hn = jnp.where(s < pl.num_programs(1) - 1, _rms(xn_ref[0], g), 0.0)
    ext_ref[0:POOL_HALO, :] = hp
    ext_ref[POOL_HALO:POOL_HALO + ts, :] = h
    ext_ref[POOL_HALO + ts:2 * POOL_HALO + ts, :] = hn
    t = s * ts + lax.broadcasted_iota(jnp.int32, (ts, 1), 0)
    for gi, w in enumerate(POOL_WINDOWS):
        lanes = slice(gi * POOL_GROUP, (gi + 1) * POOL_GROUP)
        acc = None
        for off in range(-(w // 2), w // 2):
            v = ext_ref[POOL_HALO + off:POOL_HALO + off + ts, lanes]
            acc = v if acc is None else acc + v
        lo = jnp.maximum(t - w // 2, 0)
        hi = jnp.minimum(t + w // 2 - 1, seq - 1)
        cnt = (hi - lo + 1).astype(F32)
        pooled = (acc / cnt - h[:, lanes]).astype(BF16)
        mixed = jnp.dot(pooled, w_ref[gi], preferred_element_type=F32)
        o_ref[0, :, lanes] = x[:, lanes] + mixed * sc_ref[:, lanes]


def _pool_layer(x, g, w, sc):
    b, seq, d = x.shape
    ts = TOKEN_TILE
    halo_blocks = ts // POOL_HALO
    n_halo = seq // POOL_HALO
    return pl.pallas_call(
        functools.partial(_pool_kernel, ts=ts, seq=seq),
        grid=(b, seq // ts),
        in_specs=[
            pl.BlockSpec((1, ts, d), lambda i, j: (i, j, 0)),
            pl.BlockSpec((1, POOL_HALO, d),
                         lambda i, j: (i, jnp.maximum(j * halo_blocks - 1, 0), 0)),
            pl.BlockSpec((1, POOL_HALO, d),
                         lambda i, j: (i, jnp.minimum((j + 1) * halo_blocks, n_halo - 1), 0)),
            _const_spec((1, d)),
            _const_spec((N_POOL_GROUPS, POOL_GROUP, POOL_GROUP)),
            _const_spec((1, d)),
        ],
        out_specs=pl.BlockSpec((1, ts, d), lambda i, j: (i, j, 0)),
        out_shape=jax.ShapeDtypeStruct(x.shape, F32),
        scratch_shapes=[pltpu.VMEM((ts + 2 * POOL_HALO, d), F32)],
        compiler_params=_params(2),
        name="pool_mixer",
    )(x, x, x, g, w, sc)


def _qkv_kernel(x_ref, g_ref, w_ref, o_ref):
    h = _rms(x_ref[0], g_ref[...]).astype(BF16)
    for j in range(3 * N_HEAD_BLOCKS):
        r = jnp.dot(h, w_ref[:, j * LANES:(j + 1) * LANES], preferred_element_type=F32)
        if j < N_HEAD_BLOCKS:
            r = r * (HEAD_DIM ** -0.5)
        o_ref[0, j] = r.astype(BF16)


def _qkv_layer(x, g, w):
    b, seq, d = x.shape
    tm = TOKEN_TILE
    return pl.pallas_call(
        _qkv_kernel,
        grid=(b, seq // tm),
        in_specs=[
            pl.BlockSpec((1, tm, d), lambda i, j: (i, j, 0)),
            _const_spec((1, d)),
            _const_spec((d, 3 * d)),
        ],
        out_specs=pl.BlockSpec((1, 3 * N_HEAD_BLOCKS, tm, LANES), lambda i, j: (i, 0, j, 0)),
        out_shape=jax.ShapeDtypeStruct((b, 3 * N_HEAD_BLOCKS, seq, LANES), BF16),
        compiler_params=_params(2),
        name="qkv_proj",
    )(x, g, w)


def _attn_kernel(q_ref, k_ref, v_ref, bias_ref, o_ref, *, rows):
    lane = lax.broadcasted_iota(jnp.int32, (GRID_W, LANES), 1)
    head0 = lane < HEAD_DIM

    def body(r, carry):
        r0 = jnp.clip(r - WIN_ROWS // 2, 0, rows - WIN_ROWS)
        q_start = pl.multiple_of(r * GRID_W, GRID_W)
        k_start = pl.multiple_of(r0 * GRID_W, GRID_W)
        qrow = q_ref[0, 0, pl.ds(q_start, GRID_W), :]
        kwin = k_ref[0, 0, pl.ds(k_start, WIN_KEYS), :]
        vwin = v_ref[0, 0, pl.ds(k_start, WIN_KEYS), :]
        zero = jnp.zeros_like(qrow)
        qblk = jnp.concatenate([jnp.where(head0, qrow, zero), jnp.where(head0, zero, qrow)], axis=0)
        sc = lax.dot_general(qblk, kwin, (((1,), (1,)), ((), ())), preferred_element_type=F32)
        sc = sc + bias_ref[0, r - r0]
        m = jnp.max(sc, axis=-1, keepdims=True)
        p = jnp.exp(sc - m)
        l = jnp.sum(p, axis=-1, keepdims=True)
        o = jnp.dot(p.astype(BF16), vwin, preferred_element_type=F32) / l
        out = jnp.where(head0, o[:GRID_W], o[GRID_W:])
        o_ref[0, 0, pl.ds(q_start, GRID_W), :] = out.astype(BF16)
        return carry

    lax.fori_loop(0, rows, body, 0)


def _attn_bias(rpb):
    v = np.arange(WIN_ROWS)[:, None]
    i = np.arange(WIN_ROWS)[None, :]
    dr = i - v + WIN_ROWS - 1
    qc = np.arange(GRID_W)[:, None]
    kc = np.arange(GRID_W)[None, :]
    dc = np.clip(kc - qc + WIN_COLS - 1, 0, 2 * WIN_COLS - 2)
    ws = np.clip(qc - WIN_COLS // 2, 0, GRID_W - WIN_COLS)
    valid = (kc >= ws) & (kc < ws + WIN_COLS)
    tbl = rpb.astype(F32)[:, dr[:, None, :, None], dc[None, :, None, :]]
    tbl = jnp.where(jnp.asarray(valid)[None, None, :, None, :], tbl, NEG_INF)
    tbl = tbl.reshape(N_HEAD_BLOCKS, HEADS_PER_BLOCK, WIN_ROWS, GRID_W, WIN_KEYS)
    tbl = jnp.transpose(tbl, (0, 2, 1, 3, 4))
    return tbl.reshape(N_HEAD_BLOCKS, WIN_ROWS, HEADS_PER_BLOCK * GRID_W, WIN_KEYS)


def _attn_layer(qkv, bias):
    b, _, seq, _ = qkv.shape
    rows = seq // GRID_W
    blk = (1, 1, seq, LANES)
    return pl.pallas_call(
        functools.partial(_attn_kernel, rows=rows),
        grid=(N_HEAD_BLOCKS, b),
        in_specs=[
            pl.BlockSpec(blk, lambda hb, i: (i, hb, 0, 0)),
            pl.BlockSpec(blk, lambda hb, i: (i, N_HEAD_BLOCKS + hb, 0, 0)),
            pl.BlockSpec(blk, lambda hb, i: (i, 2 * N_HEAD_BLOCKS + hb, 0, 0)),
            pl.BlockSpec((1, WIN_ROWS, HEADS_PER_BLOCK * GRID_W, WIN_KEYS),
                         lambda hb, i: (hb, 0, 0, 0)),
        ],
        out_specs=pl.BlockSpec(blk, lambda hb, i: (i, hb, 0, 0)),
        out_shape=jax.ShapeDtypeStruct((b, N_HEAD_BLOCKS, seq, LANES), BF16),
        compiler_params=_params(2),
        name="nbr_attention",
    )(qkv, qkv, qkv, bias)


def _mlp_kernel(*refs, with_attn, with_final):
    refs = list(refs)
    x_ref = refs.pop(0)
    x1 = x_ref[0]
    if with_attn:
        a_ref = refs.pop(0)
        wo_ref = refs.pop(0)
        attn = jnp.concatenate([a_ref[0, j] for j in range(N_HEAD_BLOCKS)], axis=-1)
        x1 = x1 + jnp.dot(attn, wo_ref[...], preferred_element_type=F32)
    gm_ref, wup_ref, wdn_ref = refs[:3]
    refs = refs[3:]
    h = _rms(x1, gm_ref[...]).astype(BF16)
    acc = x1
    for c in range(D_FF // FF_CHUNK):
        ff = slice(c * FF_CHUNK, (c + 1) * FF_CHUNK)
        up = jnp.dot(h, wup_ref[:, ff], preferred_element_type=F32)
        act = jnp.square(jnp.maximum(up, 0.0)).astype(BF16)
        acc = acc + jnp.dot(act, wdn_ref[ff, :], preferred_element_type=F32)
    if with_final:
        gf_ref = refs.pop(0)
        acc = _rms(acc, gf_ref[...])
    o_ref = refs.pop(0)
    o_ref[0] = acc


def _mlp_layer(x, gm, wup, wdn, attn=None, wo=None, gf=None):
    b, seq, d = x.shape
    tm = TOKEN_TILE
    args = [x]
    in_specs = [pl.BlockSpec((1, tm, d), lambda i, j: (i, j, 0))]
    if attn is not None:
        args += [attn, wo]
        in_specs += [
            pl.BlockSpec((1, N_HEAD_BLOCKS, tm, LANES), lambda i, j: (i, 0, j, 0)),
            _const_spec((d, d)),
        ]
    args += [gm, wup, wdn]
    in_specs += [_const_spec((1, d)), _const_spec((d, D_FF)), _const_spec((D_FF, d))]
    if gf is not None:
        args.append(gf)
        in_specs.append(_const_spec((1, d)))
    return pl.pallas_call(
        functools.partial(_mlp_kernel, with_attn=attn is not None, with_final=gf is not None),
        grid=(b, seq // tm),
        in_specs=in_specs,
        out_specs=pl.BlockSpec((1, tm, d), lambda i, j: (i, j, 0)),
        out_shape=jax.ShapeDtypeStruct(x.shape, F32),
        compiler_params=_params(2),
        name="mlp_block",
    )(*args)


def _trunk(x, p):
    depth = p["norm_mix"].shape[0]
    for i in range(depth):
        j = i // 2
        gm = p["norm_mlp"][i][None]
        gf = p["norm_final"][None] if i == depth - 1 else None
        if i % 2 == 0:
            x = _pool_layer(x, p["norm_mix"][i][None], p["pool_w"][j], p["pool_scale"][j][None])
            x = _mlp_layer(x, gm, p["w_up"][i], p["w_down"][i], gf=gf)
        else:
            qkv = _qkv_layer(x, p["norm_mix"][i][None], p["w_qkv"][j])
            attn = _attn_layer(qkv, p["attn_bias"][j])
            x = _mlp_layer(x, gm, p["w_up"][i], p["w_down"][i], attn=attn, wo=p["w_o"][j], gf=gf)
    return x


def kernel(x_prompt, x_sample, norm_mix, pool_w, pool_scale, w_qkv, rpb, w_o, norm_mlp, w_up, w_down, norm_final):
    p = {
        "norm_mix": norm_mix.astype(F32),
        "norm_mlp": norm_mlp.astype(F32),
        "norm_final": norm_final.astype(F32),
        "pool_scale": pool_scale.astype(F32),
        "pool_w": pool_w.astype(BF16),
        "w_qkv": w_qkv.astype(BF16),
        "w_o": w_o.astype(BF16),
        "w_up": w_up.astype(BF16),
        "w_down": w_down.astype(BF16),
        "attn_bias": jnp.stack([_attn_bias(rpb[j]) for j in range(rpb.shape[0])]),
    }
    return (_trunk(x_prompt, p), _trunk(x_sample, p))
```

```python
import functools

import numpy as np
import jax
import jax.numpy as jnp
from jax import lax
from jax.experimental import pallas as pl
from jax.experimental.pallas import tpu as pltpu

D_MODEL = 1024
D_FF = 4 * D_MODEL
GRID_W = 64
POOL_WINDOWS = (2, 4, 8, 16)
N_POOL_GROUPS = len(POOL_WINDOWS)
POOL_GROUP = D_MODEL // N_POOL_GROUPS
POOL_HALO = 8
N_HEADS = 16
HEAD_DIM = D_MODEL // N_HEADS
HEADS_PER_BLOCK = 2
N_HEAD_BLOCKS = N_HEADS // HEADS_PER_BLOCK
LANES = HEADS_PER_BLOCK * HEAD_DIM
WIN_ROWS = 8
WIN_COLS = 16
WIN_KEYS = WIN_ROWS * GRID_W
RMS_EPS = 1e-6
NEG_INF = -1e30

ATTN_ROWS_PER_ITER = 8
TOKEN_TILE = 512
FF_CHUNK = 1024
VMEM_LIMIT_BYTES = 56 * 1024 * 1024

BF16 = jnp.bfloat16
F32 = jnp.float32


def _rms(x, g):
    ms = jnp.mean(x * x, axis=-1, keepdims=True)
    return x * lax.rsqrt(ms + RMS_EPS) * g


def _params(n_grid):
    return pltpu.CompilerParams(
        dimension_semantics=("arbitrary",) * n_grid,
        vmem_limit_bytes=VMEM_LIMIT_BYTES,
    )


def _const_spec(shape):
    nd = len(shape)
    return pl.BlockSpec(shape, lambda *_: (0,) * nd, pipeline_mode=pl.Buffered(1))


def _pool_kernel(x_ref, xp_ref, xn_ref, g_ref, w_ref, sc_ref, o_ref, ext_ref, *, ts, seq):
    s = pl.program_id(1)
    g = g_ref[...]
    x = x_ref[0]
    h = _rms(x, g)
    hp = jnp.where(s > 0, _rms(xp_ref[0], g), 0.0)
    hn = jnp.where(s < pl.num_programs(1) - 1, _rms(xn_ref[0], g), 0.0)
    ext_ref[0:POOL_HALO, :] = hp
    ext_ref[POOL_HALO:POOL_HALO + ts, :] = h
    ext_ref[POOL_HALO + ts:2 * POOL_HALO + ts, :] = hn
    n_ext = ts + 2 * POOL_HALO
    t = s * ts + lax.broadcasted_iota(jnp.int32, (ts, 1), 0)
    for gi, w in enumerate(POOL_WINDOWS):
        lanes = slice(gi * POOL_GROUP, (gi + 1) * POOL_GROUP)
        acc = ext_ref[:, lanes]
        span = 1
        while span < w:
            acc = acc + pltpu.roll(acc, span, axis=0)
            span *= 2
        lead = w // 2 - 1
        if lead:
            acc = pltpu.roll(acc, n_ext - lead, axis=0)
        acc = acc[POOL_HALO:POOL_HALO + ts]
        lo = jnp.maximum(t - w // 2, 0)
        hi = jnp.minimum(t + w // 2 - 1, seq - 1)
        inv_cnt = 1.0 / (hi - lo + 1).astype(F32)
        pooled = (acc * inv_cnt - h[:, lanes]).astype(BF16)
        mixed = jnp.dot(pooled, w_ref[gi], preferred_element_type=F32)
        o_ref[0, :, lanes] = x[:, lanes] + mixed * sc_ref[:, lanes]


def _pool_layer(x, g, w, sc):
    b, seq, d = x.shape
    ts = TOKEN_TILE
    halo_blocks = ts // POOL_HALO
    n_halo = seq // POOL_HALO
    return pl.pallas_call(
        functools.partial(_pool_kernel, ts=ts, seq=seq),
        grid=(b, seq // ts),
        in_specs=[
            pl.BlockSpec((1, ts, d), lambda i, j: (i, j, 0)),
            pl.BlockSpec((1, POOL_HALO, d),
                         lambda i, j: (i, jnp.maximum(j * halo_blocks - 1, 0), 0)),
            pl.BlockSpec((1, POOL_HALO, d),
                         lambda i, j: (i, jnp.minimum((j + 1) * halo_blocks, n_halo - 1), 0)),
            _const_spec((1, d)),
            _const_spec((N_POOL_GROUPS, POOL_GROUP, POOL_GROUP)),
            _const_spec((1, d)),
        ],
        out_specs=pl.BlockSpec((1, ts, d), lambda i, j: (i, j, 0)),
        out_shape=jax.ShapeDtypeStruct(x.shape, F32),
        scratch_shapes=[pltpu.VMEM((ts + 2 * POOL_HALO, d), F32)],
        compiler_params=_params(2),
        name="pool_mixer",
    )(x, x, x, g, w, sc)


def _qkv_kernel(x_ref, g_ref, w_ref, o_ref):
    h = _rms(x_ref[0], g_ref[...]).astype(BF16)
    for part in range(3):
        r = jnp.dot(h, w_ref[:, part * D_MODEL:(part + 1) * D_MODEL], preferred_element_type=F32)
        if part == 0:
            r = r * (HEAD_DIM ** -0.5)
        r = r.astype(BF16)
        for j in range(N_HEAD_BLOCKS):
            o_ref[0, part * N_HEAD_BLOCKS + j] = r[:, j * LANES:(j + 1) * LANES]


def _qkv_layer(x, g, w):
    b, seq, d = x.shape
    tm = TOKEN_TILE
    return pl.pallas_call(
        _qkv_kernel,
        grid=(b, seq // tm),
        in_specs=[
            pl.BlockSpec((1, tm, d), lambda i, j: (i, j, 0)),
            _const_spec((1, d)),
            _const_spec((d, 3 * d)),
        ],
        out_specs=pl.BlockSpec((1, 3 * N_HEAD_BLOCKS, tm, LANES), lambda i, j: (i, 0, j, 0)),
        out_shape=jax.ShapeDtypeStruct((b, 3 * N_HEAD_BLOCKS, seq, LANES), BF16),
        compiler_params=_params(2),
        name="qkv_proj",
    )(x, g, w)


def _attn_kernel(q_ref, k_ref, v_ref, bias_ref, o_ref, *, rows):
    lane = lax.broadcasted_iota(jnp.int32, (GRID_W, LANES), 1)
    head0 = lane < HEAD_DIM

    def scores(r):
        r0 = jnp.clip(r - WIN_ROWS // 2, 0, rows - WIN_ROWS)
        q_start = pl.multiple_of(r * GRID_W, GRID_W)
        k_start = pl.multiple_of(r0 * GRID_W, GRID_W)
        qrow = q_ref[0, 0, pl.ds(q_start, GRID_W), :]
        kwin = k_ref[0, 0, pl.ds(k_start, WIN_KEYS), :]
        zero = jnp.zeros_like(qrow)
        qblk = jnp.concatenate([jnp.where(head0, qrow, zero), jnp.where(head0, zero, qrow)], axis=0)
        sc = lax.dot_general(qblk, kwin, (((1,), (1,)), ((), ())), preferred_element_type=F32)
        return sc, r - r0, q_start, k_start

    def softmax(sc, variant):
        sc = sc + bias_ref[0, variant]
        m = jnp.max(sc, axis=-1, keepdims=True)
        p = jnp.exp(sc - m)
        return p.astype(BF16), 1.0 / jnp.sum(p, axis=-1, keepdims=True)

    def weighted_values(p, inv_l, q_start, k_start):
        vwin = v_ref[0, 0, pl.ds(k_start, WIN_KEYS), :]
        o = jnp.dot(p, vwin, preferred_element_type=F32) * inv_l
        out = jnp.where(head0, o[:GRID_W], o[GRID_W:])
        o_ref[0, 0, pl.ds(q_start, GRID_W), :] = out.astype(BF16)

    def body(it, carry):
        group = [scores(it * ATTN_ROWS_PER_ITER + u) for u in range(ATTN_ROWS_PER_ITER)]
        probs = [softmax(sc, variant) for sc, variant, _, _ in group]
        for (p, inv_l), (_, _, q_start, k_start) in zip(probs, group):
            weighted_values(p, inv_l, q_start, k_start)
        return carry

    lax.fori_loop(0, rows // ATTN_ROWS_PER_ITER, body, 0)


def _attn_bias(rpb):
    n_dr = 2 * WIN_ROWS - 1
    edge = GRID_W - WIN_COLS
    rpb = rpb.astype(F32)
    ext = jnp.concatenate([jnp.repeat(rpb[..., :1], edge, axis=-1), rpb,
                           jnp.repeat(rpb[..., -1:], edge + 1, axis=-1)], axis=-1)
    width = 2 * GRID_W
    flat = jnp.tile(ext, (1, 1, GRID_W))
    flat = flat[..., GRID_W - 1:GRID_W - 1 + GRID_W * (width - 1)]
    toe = flat.reshape(N_HEADS, n_dr, GRID_W, width - 1)[..., :GRID_W]
    qc = np.arange(GRID_W)[:, None]
    kc = np.arange(GRID_W)[None, :]
    ws = np.clip(qc - WIN_COLS // 2, 0, GRID_W - WIN_COLS)
    valid = (kc >= ws) & (kc < ws + WIN_COLS)
    toe = jnp.where(jnp.asarray(valid)[None, None], toe, NEG_INF)
    tbl = jnp.stack([toe[:, WIN_ROWS - 1 - v:2 * WIN_ROWS - 1 - v] for v in range(WIN_ROWS)], axis=1)
    tbl = tbl.reshape(N_HEAD_BLOCKS, HEADS_PER_BLOCK, WIN_ROWS, WIN_ROWS, GRID_W, GRID_W)
    tbl = jnp.transpose(tbl, (0, 2, 1, 4, 3, 5))
    return tbl.reshape(N_HEAD_BLOCKS, WIN_ROWS, HEADS_PER_BLOCK * GRID_W, WIN_KEYS)


def _attn_layer(qkv, bias):
    b, _, seq, _ = qkv.shape
    rows = seq // GRID_W
    blk = (1, 1, seq, LANES)
    return pl.pallas_call(
        functools.partial(_attn_kernel, rows=rows),
        grid=(N_HEAD_BLOCKS, b),
        in_specs=[
            pl.BlockSpec(blk, lambda hb, i: (i, hb, 0, 0)),
            pl.BlockSpec(blk, lambda hb, i: (i, N_HEAD_BLOCKS + hb, 0, 0)),
            pl.BlockSpec(blk, lambda hb, i: (i, 2 * N_HEAD_BLOCKS + hb, 0, 0)),
            pl.BlockSpec((1, WIN_ROWS, HEADS_PER_BLOCK * GRID_W, WIN_KEYS),
                         lambda hb, i: (hb, 0, 0, 0)),
        ],
        out_specs=pl.BlockSpec(blk, lambda hb, i: (i, hb, 0, 0)),
        out_shape=jax.ShapeDtypeStruct((b, N_HEAD_BLOCKS, seq, LANES), BF16),
        compiler_params=_params(2),
        name="nbr_attention",
    )(qkv, qkv, qkv, bias)


def _mlp_kernel(*refs, with_attn, with_final):
    refs = list(refs)
    x_ref = refs.pop(0)
    x1 = x_ref[0]
    if with_attn:
        a_ref = refs.pop(0)
        wo_ref = refs.pop(0)
        attn = jnp.concatenate([a_ref[0, j] for j in range(N_HEAD_BLOCKS)], axis=-1)
        x1 = x1 + jnp.dot(attn, wo_ref[...], preferred_element_type=F32)
    gm_ref, wup_ref, wdn_ref = refs[:3]
    refs = refs[3:]
    h = _rms(x1, gm_ref[...]).astype(BF16)
    acc = x1
    for c in range(D_FF // FF_CHUNK):
        ff = slice(c * FF_CHUNK, (c + 1) * FF_CHUNK)
        up = jnp.dot(h, wup_ref[:, ff], preferred_element_type=F32)
        act = jnp.square(jnp.maximum(up, 0.0)).astype(BF16)
        acc = acc + jnp.dot(act, wdn_ref[ff, :], preferred_element_type=F32)
    if with_final:
        gf_ref = refs.pop(0)
        acc = _rms(acc, gf_ref[...])
    o_ref = refs.pop(0)
    o_ref[0] = acc


def _mlp_layer(x, gm, wup, wdn, attn=None, wo=None, gf=None):
    b, seq, d = x.shape
    tm = TOKEN_TILE
    args = [x]
    in_specs = [pl.BlockSpec((1, tm, d), lambda i, j: (i, j, 0))]
    if attn is not None:
        args += [attn, wo]
        in_specs += [
            pl.BlockSpec((1, N_HEAD_BLOCKS, tm, LANES), lambda i, j: (i, 0, j, 0)),
            _const_spec((d, d)),
        ]
    args += [gm, wup, wdn]
    in_specs += [_const_spec((1, d)), _const_spec((d, D_FF)), _const_spec((D_FF, d))]
    if gf is not None:
        args.append(gf)
        in_specs.append(_const_spec((1, d)))
    return pl.pallas_call(
        functools.partial(_mlp_kernel, with_attn=attn is not None, with_final=gf is not None),
        grid=(b, seq // tm),
        in_specs=in_specs,
        out_specs=pl.BlockSpec((1, tm, d), lambda i, j: (i, j, 0)),
        out_shape=jax.ShapeDtypeStruct(x.shape, F32),
        compiler_params=_params(2),
        name="mlp_block",
    )(*args)


def _trunk(x, p):
    depth = p["norm_mix"].shape[0]
    for i in range(depth):
        j = i // 2
        gm = p["norm_mlp"][i][None]
        gf = p["norm_final"][None] if i == depth - 1 else None
        if i % 2 == 0:
            x = _pool_layer(x, p["norm_mix"][i][None], p["pool_w"][j], p["pool_scale"][j][None])
            x = _mlp_layer(x, gm, p["w_up"][i], p["w_down"][i], gf=gf)
        else:
            qkv = _qkv_layer(x, p["norm_mix"][i][None], p["w_qkv"][j])
            attn = _attn_layer(qkv, p["attn_bias"][j])
            x = _mlp_layer(x, gm, p["w_up"][i], p["w_down"][i], attn=attn, wo=p["w_o"][j], gf=gf)
    return x


def kernel(x_prompt, x_sample, norm_mix, pool_w, pool_scale, w_qkv, rpb, w_o, norm_mlp, w_up, w_down, norm_final):
    p = {
        "norm_mix": norm_mix.astype(F32),
        "norm_mlp": norm_mlp.astype(F32),
        "norm_final": norm_final.astype(F32),
        "pool_scale": pool_scale.astype(F32),
        "pool_w": pool_w.astype(BF16),
        "w_qkv": w_qkv.astype(BF16),
        "w_o": w_o.astype(BF16),
        "w_up": w_up.astype(BF16),
        "w_down": w_down.astype(BF16),
        "attn_bias": jnp.stack([_attn_bias(rpb[j]) for j in range(rpb.shape[0])]),
    }
    return (_trunk(x_prompt, p), _trunk(x_sample, p))
```

```python
import functools

import numpy as np
import jax
import jax.numpy as jnp
from jax import lax
from jax.experimental import pallas as pl
from jax.experimental.pallas import tpu as pltpu

D_MODEL = 1024
D_FF = 4 * D_MODEL
GRID_W = 64
POOL_WINDOWS = (2, 4, 8, 16)
N_POOL_GROUPS = len(POOL_WINDOWS)
POOL_GROUP = D_MODEL // N_POOL_GROUPS
POOL_HALO = 8
N_HEADS = 16
HEAD_DIM = D_MODEL // N_HEADS
HEADS_PER_BLOCK = 2
N_HEAD_BLOCKS = N_HEADS // HEADS_PER_BLOCK
LANES = HEADS_PER_BLOCK * HEAD_DIM
QBLK_ROWS = HEADS_PER_BLOCK * GRID_W
WIN_ROWS = 8
WIN_COLS = 16
WIN_KEYS = WIN_ROWS * GRID_W
KEY_TILE_COLS = 16
N_KEY_TILES = GRID_W // KEY_TILE_COLS
TILE_KEYS = WIN_ROWS * KEY_TILE_COLS
SLAB_KEYS = 2 * TILE_KEYS
Q_SLABS = ((0, 24, 0), (24, 16, 1), (40, 24, 2))
for _q_lo, _n, _ft in Q_SLABS:
    _ws = np.clip(np.arange(_q_lo, _q_lo + _n) - WIN_COLS // 2, 0, GRID_W - WIN_COLS)
    assert _ws.min() >= _ft * KEY_TILE_COLS and _ws.max() + WIN_COLS <= (_ft + 2) * KEY_TILE_COLS
RMS_EPS = 1e-6
NEG_INF = -1e30

ATTN_ROWS_PER_ITER = 8
ATTN_TOKENS_PER_STEP = 8192
TOKEN_TILE = 512
FF_CHUNK = 1024
VMEM_LIMIT_BYTES = 56 * 1024 * 1024

BF16 = jnp.bfloat16
F32 = jnp.float32


def _rms(x, g):
    ms = jnp.mean(x * x, axis=-1, keepdims=True)
    return x * lax.rsqrt(ms + RMS_EPS) * g


def _params(n_grid):
    return pltpu.CompilerParams(
        dimension_semantics=("arbitrary",) * n_grid,
        vmem_limit_bytes=VMEM_LIMIT_BYTES,
    )


def _const_spec(shape):
    nd = len(shape)
    return pl.BlockSpec(shape, lambda *_: (0,) * nd, pipeline_mode=pl.Buffered(1))


def _pool_kernel(x_ref, xp_ref, xn_ref, g_ref, w_ref, sc_ref, o_ref, ext_ref, *, ts, seq):
    s = pl.program_id(1)
    g = g_ref[...]
    x = x_ref[0]
    h = _rms(x, g)
    hp = jnp.where(s > 0, _rms(xp_ref[0], g), 0.0)
    hn = jnp.where(s < pl.num_programs(1) - 1, _rms(xn_ref[0], g), 0.0)
    ext_ref[0:POOL_HALO, :] = hp
    ext_ref[POOL_HALO:POOL_HALO + ts, :] = h
    ext_ref[POOL_HALO + ts:2 * POOL_HALO + ts, :] = hn
    n_ext = ts + 2 * POOL_HALO
    edge_row = lax.broadcasted_iota(jnp.int32, (POOL_HALO, 1), 0)
    t_edges = (s * ts + edge_row, s * ts + (ts - POOL_HALO) + edge_row)
    for gi, w in enumerate(POOL_WINDOWS):
        lanes = slice(gi * POOL_GROUP, (gi + 1) * POOL_GROUP)
        acc = ext_ref[:, lanes]
        span = 1
        while span < w:
            acc = acc + pltpu.roll(acc, span, axis=0)
            span *= 2
        lead = w // 2 - 1
        if lead:
            acc = pltpu.roll(acc, n_ext - lead, axis=0)
        acc = acc[POOL_HALO:POOL_HALO + ts]
        inv_edges = []
        for t in t_edges:
            lo = jnp.maximum(t - w // 2, 0)
            hi = jnp.minimum(t + w // 2 - 1, seq - 1)
            inv_edges.append(1.0 / (hi - lo + 1).astype(F32))
        mean = jnp.concatenate([acc[:POOL_HALO] * inv_edges[0],
                                acc[POOL_HALO:ts - POOL_HALO] * (1.0 / w),
                                acc[ts - POOL_HALO:] * inv_edges[1]], axis=0)
        pooled = (mean - h[:, lanes]).astype(BF16)
        mixed = jnp.dot(pooled, w_ref[gi], preferred_element_type=F32)
        o_ref[0, :, lanes] = x[:, lanes] + mixed * sc_ref[:, lanes]


def _pool_layer(x, g, w, sc):
    b, seq, d = x.shape
    ts = TOKEN_TILE
    halo_blocks = ts // POOL_HALO
    n_halo = seq // POOL_HALO
    return pl.pallas_call(
        functools.partial(_pool_kernel, ts=ts, seq=seq),
        grid=(b, seq // ts),
        in_specs=[
            pl.BlockSpec((1, ts, d), lambda i, j: (i, j, 0)),
            pl.BlockSpec((1, POOL_HALO, d),
                         lambda i, j: (i, jnp.maximum(j * halo_blocks - 1, 0), 0)),
            pl.BlockSpec((1, POOL_HALO, d),
                         lambda i, j: (i, jnp.minimum((j + 1) * halo_blocks, n_halo - 1), 0)),
            _const_spec((1, d)),
            _const_spec((N_POOL_GROUPS, POOL_GROUP, POOL_GROUP)),
            _const_spec((1, d)),
        ],
        out_specs=pl.BlockSpec((1, ts, d), lambda i, j: (i, j, 0)),
        out_shape=jax.ShapeDtypeStruct(x.shape, F32),
        scratch_shapes=[pltpu.VMEM((ts + 2 * POOL_HALO, d), F32)],
        compiler_params=_params(2),
        name="pool_mixer",
    )(x, x, x, g, w, sc)


def _qkv_kernel(x_ref, g_ref, w_ref, q_ref, kv_ref, *, tm):
    h = _rms(x_ref[0], g_ref[...]).astype(BF16)
    grid_rows = tm // GRID_W
    lane = lax.broadcasted_iota(jnp.int32, (1, D_MODEL), 1)
    head0 = (lane % LANES) < HEAD_DIM

    q = jnp.dot(h, w_ref[:, :D_MODEL], preferred_element_type=F32) * (HEAD_DIM ** -0.5)
    pieces = []
    for row in range(grid_rows):
        for q_lo, n, _ in Q_SLABS:
            part = q[row * GRID_W + q_lo:row * GRID_W + q_lo + n]
            pieces += [jnp.where(head0, part, 0.0), jnp.where(head0, 0.0, part)]
    q = jnp.concatenate(pieces, axis=0).astype(BF16)
    for j in range(N_HEAD_BLOCKS):
        q_ref[0, j] = q[:, j * LANES:(j + 1) * LANES]

    for part in range(2):
        cols = slice((part + 1) * D_MODEL, (part + 2) * D_MODEL)
        r = jnp.dot(h, w_ref[:, cols], preferred_element_type=F32).astype(BF16)
        for t in range(N_KEY_TILES):
            tile = jnp.concatenate(
                [r[row * GRID_W + t * KEY_TILE_COLS:row * GRID_W + (t + 1) * KEY_TILE_COLS]
                 for row in range(grid_rows)], axis=0)
            for j in range(N_HEAD_BLOCKS):
                kv_ref[0, part * N_HEAD_BLOCKS + j, t] = tile[:, j * LANES:(j + 1) * LANES]


def _qkv_layer(x, g, w):
    b, seq, d = x.shape
    tm = TOKEN_TILE
    return pl.pallas_call(
        functools.partial(_qkv_kernel, tm=tm),
        grid=(b, seq // tm),
        in_specs=[
            pl.BlockSpec((1, tm, d), lambda i, j: (i, j, 0)),
            _const_spec((1, d)),
            _const_spec((d, 3 * d)),
        ],
        out_specs=[
            pl.BlockSpec((1, N_HEAD_BLOCKS, HEADS_PER_BLOCK * tm, LANES), lambda i, j: (i, 0, j, 0)),
            pl.BlockSpec((1, 2 * N_HEAD_BLOCKS, N_KEY_TILES, tm // N_KEY_TILES, LANES),
                         lambda i, j: (i, 0, 0, j, 0)),
        ],
        out_shape=[
            jax.ShapeDtypeStruct((b, N_HEAD_BLOCKS, HEADS_PER_BLOCK * seq, LANES), BF16),
            jax.ShapeDtypeStruct((b, 2 * N_HEAD_BLOCKS, N_KEY_TILES, seq // N_KEY_TILES, LANES), BF16),
        ],
        compiler_params=_params(2),
        name="qkv_proj",
    )(x, g, w)


def _slab_rows():
    out, lo = [], 0
    for _, n, ft in Q_SLABS:
        out.append((lo, HEADS_PER_BLOCK * n, n, ft))
        lo += HEADS_PER_BLOCK * n
    return out


def _attn_kernel(q_ref, k_ref, v_ref, bias_ref, o_ref, pa_ref, pb_ref, *, rows, nb):
    groups_per_seq = rows // ATTN_ROWS_PER_ITER
    n_groups = nb * groups_per_seq
    assert n_groups % 2 == 0
    slabs = _slab_rows()
    ones = jnp.ones((WIN_KEYS, LANES), BF16)

    def locate(g, u):
        bi = g // groups_per_seq
        r = (g % groups_per_seq) * ATTN_ROWS_PER_ITER + u
        r0 = jnp.clip(r - WIN_ROWS // 2, 0, rows - WIN_ROWS)
        return bi, r, r - r0, pl.multiple_of(r0 * KEY_TILE_COLS, KEY_TILE_COLS)

    def window(ref, bi, k_start):
        return jnp.concatenate([ref[bi, 0, t, pl.ds(k_start, TILE_KEYS), :] for t in range(N_KEY_TILES)],
                               axis=0)

    def probabilities(g, p_ref):
        where = [locate(g, u) for u in range(ATTN_ROWS_PER_ITER)]
        scores = []
        for bi, r, _, k_start in where:
            qblk = q_ref[bi, 0, pl.ds(pl.multiple_of(r * QBLK_ROWS, QBLK_ROWS), QBLK_ROWS), :]
            scores.append(lax.dot_general(qblk, window(k_ref, bi, k_start), (((1,), (1,)), ((), ())),
                                          preferred_element_type=F32))
        for u, (sc, (_, _, variant, _)) in enumerate(zip(scores, where)):
            sc = jnp.concatenate([sc[lo:lo + n2, ft * TILE_KEYS:ft * TILE_KEYS + SLAB_KEYS]
                                  for lo, n2, _, ft in slabs], axis=0)
            sc = sc + bias_ref[0, variant]
            m = jnp.max(sc, axis=-1, keepdims=True)
            p_ref[u] = jnp.exp(sc - m).astype(BF16)

    def weighted_values(g, p_ref):
        head0 = lax.broadcasted_iota(jnp.int32, (1, LANES), 1) < HEAD_DIM
        for u in range(ATTN_ROWS_PER_ITER):
            bi, r, _, k_start = locate(g, u)
            p = p_ref[u]
            full = []
            for lo, n2, _, ft in slabs:
                zero = jnp.zeros((n2, TILE_KEYS), BF16)
                full.append(jnp.concatenate(
                    [zero] * ft + [p[lo:lo + n2, :TILE_KEYS], p[lo:lo + n2, TILE_KEYS:]]
                    + [zero] * (N_KEY_TILES - 2 - ft), axis=1))
            p = jnp.concatenate(full, axis=0)
            vaug = jnp.concatenate([window(v_ref, bi, k_start), ones], axis=1)
            o = jnp.dot(p, vaug, preferred_element_type=F32)
            o = o[:, :LANES] * (1.0 / o[:, LANES:])
            out = jnp.concatenate([jnp.where(head0, o[lo:lo + n], o[lo + n:lo + 2 * n])
                                   for lo, _, n, _ in slabs], axis=0)
            o_ref[bi, 0, pl.ds(pl.multiple_of(r * GRID_W, GRID_W), GRID_W), :] = out.astype(BF16)

    probabilities(0, pa_ref)

    def body(it, carry):
        g = 2 * it + 1
        probabilities(g, pb_ref)
        weighted_values(g - 1, pa_ref)
        probabilities(g + 1, pa_ref)
        weighted_values(g, pb_ref)
        return carry

    lax.fori_loop(0, n_groups // 2 - 1, body, 0)
    probabilities(n_groups - 1, pb_ref)
    weighted_values(n_groups - 2, pa_ref)
    weighted_values(n_groups - 1, pb_ref)


def _attn_bias(rpb):
    n_dr = 2 * WIN_ROWS - 1
    edge = GRID_W - WIN_COLS
    rpb = rpb.astype(F32)
    ext = jnp.concatenate([jnp.repeat(rpb[..., :1], edge, axis=-1), rpb,
                           jnp.repeat(rpb[..., -1:], edge + 1, axis=-1)], axis=-1)
    width = 2 * GRID_W
    flat = jnp.tile(ext, (1, 1, GRID_W))
    flat = flat[..., GRID_W - 1:GRID_W - 1 + GRID_W * (width - 1)]
    toe = flat.reshape(N_HEADS, n_dr, GRID_W, width - 1)[..., :GRID_W]
    qc = np.arange(GRID_W)[:, None]
    kc = np.arange(GRID_W)[None, :]
    ws = np.clip(qc - WIN_COLS // 2, 0, GRID_W - WIN_COLS)
    valid = (kc >= ws) & (kc < ws + WIN_COLS)
    toe = jnp.where(jnp.asarray(valid)[None, None], toe, NEG_INF)
    slabs = []
    for q_lo, n, ft in Q_SLABS:
        c_lo = ft * KEY_TILE_COLS
        part = toe[:, :, q_lo:q_lo + n, c_lo:c_lo + 2 * KEY_TILE_COLS]
        part = jnp.stack([part[:, WIN_ROWS - 1 - v:2 * WIN_ROWS - 1 - v] for v in range(WIN_ROWS)], axis=1)
        part = part.reshape(N_HEAD_BLOCKS, HEADS_PER_BLOCK, WIN_ROWS, WIN_ROWS, n, 2, KEY_TILE_COLS)
        part = jnp.transpose(part, (0, 2, 1, 4, 5, 3, 6))
        slabs.append(part.reshape(N_HEAD_BLOCKS, WIN_ROWS, HEADS_PER_BLOCK * n, SLAB_KEYS))
    return jnp.concatenate(slabs, axis=2)


def _attn_layer(q, kv, bias):
    b, _, seq2, _ = q.shape
    seq = seq2 // HEADS_PER_BLOCK
    rows = seq // GRID_W
    nb = min(ATTN_TOKENS_PER_STEP // seq, b)
    assert b % nb == 0
    kv_blk = (nb, 1, N_KEY_TILES, seq // N_KEY_TILES, LANES)
    p_scratch = pltpu.VMEM((ATTN_ROWS_PER_ITER, QBLK_ROWS, SLAB_KEYS), BF16)
    return pl.pallas_call(
        functools.partial(_attn_kernel, rows=rows, nb=nb),
        grid=(N_HEAD_BLOCKS, b // nb),
        scratch_shapes=[p_scratch, p_scratch],
        in_specs=[
            pl.BlockSpec((nb, 1, seq2, LANES), lambda hb, i: (i, hb, 0, 0)),
            pl.BlockSpec(kv_blk, lambda hb, i: (i, hb, 0, 0, 0)),
            pl.BlockSpec(kv_blk, lambda hb, i: (i, N_HEAD_BLOCKS + hb, 0, 0, 0)),
            pl.BlockSpec((1, WIN_ROWS, QBLK_ROWS, SLAB_KEYS), lambda hb, i: (hb, 0, 0, 0)),
        ],
        out_specs=pl.BlockSpec((nb, 1, seq, LANES), lambda hb, i: (i, hb, 0, 0)),
        out_shape=jax.ShapeDtypeStruct((b, N_HEAD_BLOCKS, seq, LANES), BF16),
        compiler_params=_params(2),
        name="nbr_attention",
    )(q, kv, kv, bias)


def _mlp_kernel(*refs, with_attn, with_final):
    refs = list(refs)
    x_ref = refs.pop(0)
    x1 = x_ref[0]
    if with_attn:
        a_ref = refs.pop(0)
        wo_ref = refs.pop(0)
        attn = jnp.concatenate([a_ref[0, j] for j in range(N_HEAD_BLOCKS)], axis=-1)
        x1 = x1 + jnp.dot(attn, wo_ref[...], preferred_element_type=F32)
    gm_ref, wup_ref, wdn_ref = refs[:3]
    refs = refs[3:]
    h = _rms(x1, gm_ref[...]).astype(BF16)
    acc = x1
    for c in range(D_FF // FF_CHUNK):
        ff = slice(c * FF_CHUNK, (c + 1) * FF_CHUNK)
        up = jnp.dot(h, wup_ref[:, ff], preferred_element_type=F32)
        act = jnp.square(jnp.maximum(up, 0.0)).astype(BF16)
        acc = acc + jnp.dot(act, wdn_ref[ff, :], preferred_element_type=F32)
    if with_final:
        gf_ref = refs.pop(0)
        acc = _rms(acc, gf_ref[...])
    o_ref = refs.pop(0)
    o_ref[0] = acc


def _mlp_layer(x, gm, wup, wdn, attn=None, wo=None, gf=None):
    b, seq, d = x.shape
    tm = TOKEN_TILE
    args = [x]
    in_specs = [pl.BlockSpec((1, tm, d), lambda i, j: (i, j, 0))]
    if attn is not None:
        args += [attn, wo]
        in_specs += [
            pl.BlockSpec((1, N_HEAD_BLOCKS, tm, LANES), lambda i, j: (i, 0, j, 0)),
            _const_spec((d, d)),
        ]
    args += [gm, wup, wdn]
    in_specs += [_const_spec((1, d)), _const_spec((d, D_FF)), _const_spec((D_FF, d))]
    if gf is not None:
        args.append(gf)
        in_specs.append(_const_spec((1, d)))
    return pl.pallas_call(
        functools.partial(_mlp_kernel, with_attn=attn is not None, with_final=gf is not None),
        grid=(b, seq // tm),
        in_specs=in_specs,
        out_specs=pl.BlockSpec((1, tm, d), lambda i, j: (i, j, 0)),
        out_shape=jax.ShapeDtypeStruct(x.shape, F32),
        compiler_params=_params(2),
        name="mlp_block",
    )(*args)


def _trunk(x, p):
    depth = p["norm_mix"].shape[0]
    for i in range(depth):
        j = i // 2
        gm = p["norm_mlp"][i][None]
        gf = p["norm_final"][None] if i == depth - 1 else None
        if i % 2 == 0:
            x = _pool_layer(x, p["norm_mix"][i][None], p["pool_w"][j], p["pool_scale"][j][None])
            x = _mlp_layer(x, gm, p["w_up"][i], p["w_down"][i], gf=gf)
        else:
            q, kv = _qkv_layer(x, p["norm_mix"][i][None], p["w_qkv"][j])
            attn = _attn_layer(q, kv, p["attn_bias"][j])
            x = _mlp_layer(x, gm, p["w_up"][i], p["w_down"][i], attn=attn, wo=p["w_o"][j], gf=gf)
    return x


def kernel(x_prompt, x_sample, norm_mix, pool_w, pool_scale, w_qkv, rpb, w_o, norm_mlp, w_up, w_down, norm_final):
    p = {
        "norm_mix": norm_mix.astype(F32),
        "norm_mlp": norm_mlp.astype(F32),
        "norm_final": norm_final.astype(F32),
        "pool_scale": pool_scale.astype(F32),
        "pool_w": pool_w.astype(BF16),
        "w_qkv": w_qkv.astype(BF16),
        "w_o": w_o.astype(BF16),
        "w_up": w_up.astype(BF16),
        "w_down": w_down.astype(BF16),
        "attn_bias": jnp.stack([_attn_bias(rpb[j]) for j in range(rpb.shape[0])]),
    }
    return (_trunk(x_prompt, p), _trunk(x_sample, p))
```

```python
import functools

import numpy as np
import jax
import jax.numpy as jnp
from jax import lax
from jax.experimental import pallas as pl
from jax.experimental.pallas import tpu as pltpu

D_MODEL = 1024
D_FF = 4 * D_MODEL
GRID_W = 64
POOL_WINDOWS = (2, 4, 8, 16)
N_POOL_GROUPS = len(POOL_WINDOWS)
POOL_GROUP = D_MODEL // N_POOL_GROUPS
POOL_HALO = 8
N_HEADS = 16
HEAD_DIM = D_MODEL // N_HEADS
HEADS_PER_BLOCK = 2
N_HEAD_BLOCKS = N_HEADS // HEADS_PER_BLOCK
LANES = HEADS_PER_BLOCK * HEAD_DIM
QBLK_ROWS = HEADS_PER_BLOCK * GRID_W
WIN_ROWS = 8
WIN_COLS = 16
WIN_KEYS = WIN_ROWS * GRID_W
KEY_TILE_COLS = 16
N_KEY_TILES = GRID_W // KEY_TILE_COLS
TILE_KEYS = WIN_ROWS * KEY_TILE_COLS
SLAB_KEYS = 2 * TILE_KEYS
REL_LANES = 2 * TILE_KEYS
Q_SLABS = ((0, 24, 0), (24, 16, 1), (40, 24, 2))
for _q_lo, _n, _ft in Q_SLABS:
    _ws = np.clip(np.arange(_q_lo, _q_lo + _n) - WIN_COLS // 2, 0, GRID_W - WIN_COLS)
    assert _ws.min() >= _ft * KEY_TILE_COLS and _ws.max() + WIN_COLS <= (_ft + 2) * KEY_TILE_COLS
RMS_EPS = 1e-6
NEG_INF = -1e30

ATTN_ROWS_PER_ITER = 8
ATTN_TOKENS_PER_STEP = 16384
TOKEN_TILE = 512
POOL_MLP_TILE = 1024
FF_CHUNK = 1024
VMEM_LIMIT_BYTES = 56 * 1024 * 1024

BF16 = jnp.bfloat16
F32 = jnp.float32


def _rms(x, g):
    ms = jnp.mean(x * x, axis=-1, keepdims=True)
    return x * lax.rsqrt(ms + RMS_EPS) * g


def _params(n_grid):
    return pltpu.CompilerParams(
        dimension_semantics=("arbitrary",) * n_grid,
        vmem_limit_bytes=VMEM_LIMIT_BYTES,
    )


def _const_spec(shape):
    nd = len(shape)
    return pl.BlockSpec(shape, lambda *_: (0,) * nd, pipeline_mode=pl.Buffered(1))


def _pool_mix(ext_ref, x, ext_lo, t0, w_ref, sc_ref, seq):
    n = x.shape[0]
    n_ext = n + 2 * POOL_HALO
    edge_row = lax.broadcasted_iota(jnp.int32, (POOL_HALO, 1), 0)
    t_edges = (t0 + edge_row, t0 + (n - POOL_HALO) + edge_row)
    out = []
    for gi, w in enumerate(POOL_WINDOWS):
        lanes = slice(gi * POOL_GROUP, (gi + 1) * POOL_GROUP)
        acc = ext_ref[ext_lo:ext_lo + n_ext, lanes]
        span = 1
        while span < w:
            acc = acc + pltpu.roll(acc, span, axis=0)
            span *= 2
        lead = w // 2 - 1
        if lead:
            acc = pltpu.roll(acc, n_ext - lead, axis=0)
        acc = acc[POOL_HALO:POOL_HALO + n]
        inv_edges = []
        for t in t_edges:
            lo = jnp.maximum(t - w // 2, 0)
            hi = jnp.minimum(t + w // 2 - 1, seq - 1)
            inv_edges.append(1.0 / (hi - lo + 1).astype(F32))
        mean = jnp.concatenate([acc[:POOL_HALO] * inv_edges[0],
                                acc[POOL_HALO:n - POOL_HALO] * (1.0 / w),
                                acc[n - POOL_HALO:] * inv_edges[1]], axis=0)
        h = ext_ref[ext_lo + POOL_HALO:ext_lo + POOL_HALO + n, lanes]
        pooled = (mean - h).astype(BF16)
        mixed = jnp.dot(pooled, w_ref[gi], preferred_element_type=F32)
        out.append(x[:, lanes] + mixed * sc_ref[:, lanes])
    return jnp.concatenate(out, axis=1)


def _qkv_kernel(x_ref, g_ref, w_ref, q_ref, kv_ref, *, tm):
    h = _rms(x_ref[0], g_ref[...]).astype(BF16)
    grid_rows = tm // GRID_W
    lane = lax.broadcasted_iota(jnp.int32, (1, D_MODEL), 1)
    head0 = (lane % LANES) < HEAD_DIM

    q = jnp.dot(h, w_ref[:, :D_MODEL], preferred_element_type=F32) * (HEAD_DIM ** -0.5)
    pieces = []
    for row in range(grid_rows):
        for q_lo, n, _ in Q_SLABS:
            part = q[row * GRID_W + q_lo:row * GRID_W + q_lo + n]
            pieces += [jnp.where(head0, part, 0.0), jnp.where(head0, 0.0, part)]
    q = jnp.concatenate(pieces, axis=0).astype(BF16)
    for j in range(N_HEAD_BLOCKS):
        q_ref[0, j] = q[:, j * LANES:(j + 1) * LANES]

    for part in range(2):
        cols = slice((part + 1) * D_MODEL, (part + 2) * D_MODEL)
        r = jnp.dot(h, w_ref[:, cols], preferred_element_type=F32).astype(BF16)
        for t in range(N_KEY_TILES):
            tile = jnp.concatenate(
                [r[row * GRID_W + t * KEY_TILE_COLS:row * GRID_W + (t + 1) * KEY_TILE_COLS]
                 for row in range(grid_rows)], axis=0)
            for j in range(N_HEAD_BLOCKS):
                kv_ref[0, part * N_HEAD_BLOCKS + j, t] = tile[:, j * LANES:(j + 1) * LANES]


def _qkv_layer(x, g, w):
    b, seq, d = x.shape
    tm = TOKEN_TILE
    return pl.pallas_call(
        functools.partial(_qkv_kernel, tm=tm),
        grid=(b, seq // tm),
        in_specs=[
            pl.BlockSpec((1, tm, d), lambda i, j: (i, j, 0)),
            _const_spec((1, d)),
            _const_spec((d, 3 * d)),
        ],
        out_specs=[
            pl.BlockSpec((1, N_HEAD_BLOCKS, HEADS_PER_BLOCK * tm, LANES), lambda i, j: (i, 0, j, 0)),
            pl.BlockSpec((1, 2 * N_HEAD_BLOCKS, N_KEY_TILES, tm // N_KEY_TILES, LANES),
                         lambda i, j: (i, 0, 0, j, 0)),
        ],
        out_shape=[
            jax.ShapeDtypeStruct((b, N_HEAD_BLOCKS, HEADS_PER_BLOCK * seq, LANES), BF16),
            jax.ShapeDtypeStruct((b, 2 * N_HEAD_BLOCKS, N_KEY_TILES, seq // N_KEY_TILES, LANES), BF16),
        ],
        compiler_params=_params(2),
        name="qkv_proj",
    )(x, g, w)


def _slab_rows():
    out, lo = [], 0
    for _, n, ft in Q_SLABS:
        out.append((lo, HEADS_PER_BLOCK * n, n, ft))
        lo += HEADS_PER_BLOCK * n
    return out


def _attn_kernel(q_ref, k_ref, v_ref, rel_ref, o_ref, bias_ref, pa_ref, pb_ref, *, rows, nb):
    groups_per_seq = rows // ATTN_ROWS_PER_ITER
    n_groups = nb * groups_per_seq
    assert n_groups % 2 == 0
    slabs = _slab_rows()
    ones = jnp.ones((WIN_KEYS, LANES), BF16)

    @pl.when(pl.program_id(1) == 0)
    def _():
        for tile in range(2):
            rel = rel_ref[0, :, tile * REL_LANES:(tile + 1) * REL_LANES]
            for v in range(WIN_ROWS):
                shift = (WIN_ROWS - 1 - v) * KEY_TILE_COLS
                moved = pltpu.roll(rel, REL_LANES - shift, axis=1) if shift else rel
                bias_ref[v, :, tile * TILE_KEYS:(tile + 1) * TILE_KEYS] = moved[:, :TILE_KEYS]

    def locate(g, u):
        bi = g // groups_per_seq
        r = (g % groups_per_seq) * ATTN_ROWS_PER_ITER + u
        r0 = jnp.clip(r - WIN_ROWS // 2, 0, rows - WIN_ROWS)
        return bi, r, r - r0, pl.multiple_of(r0 * KEY_TILE_COLS, KEY_TILE_COLS)

    def window(ref, bi, k_start):
        return jnp.concatenate([ref[bi, 0, t, pl.ds(k_start, TILE_KEYS), :] for t in range(N_KEY_TILES)],
                               axis=0)

    def probabilities(g, p_ref):
        where = [locate(g, u) for u in range(ATTN_ROWS_PER_ITER)]
        scores = []
        for bi, r, _, k_start in where:
            qblk = q_ref[bi, 0, pl.ds(pl.multiple_of(r * QBLK_ROWS, QBLK_ROWS), QBLK_ROWS), :]
            scores.append(lax.dot_general(qblk, window(k_ref, bi, k_start), (((1,), (1,)), ((), ())),
                                          preferred_element_type=F32))
        for u, (sc, (_, _, variant, _)) in enumerate(zip(scores, where)):
            sc = jnp.concatenate([sc[lo:lo + n2, ft * TILE_KEYS:ft * TILE_KEYS + SLAB_KEYS]
                                  for lo, n2, _, ft in slabs], axis=0)
            sc = sc + bias_ref[variant]
            m = jnp.max(sc, axis=-1, keepdims=True)
            p_ref[u] = jnp.exp(sc - m).astype(BF16)

    def weighted_values(g, p_ref):
        head0 = lax.broadcasted_iota(jnp.int32, (1, LANES), 1) < HEAD_DIM
        for u in range(ATTN_ROWS_PER_ITER):
            bi, r, _, k_start = locate(g, u)
            p = p_ref[u]
            full = []
            for lo, n2, _, ft in slabs:
                zero = jnp.zeros((n2, TILE_KEYS), BF16)
                full.append(jnp.concatenate(
                    [zero] * ft + [p[lo:lo + n2, :TILE_KEYS], p[lo:lo + n2, TILE_KEYS:]]
                    + [zero] * (N_KEY_TILES - 2 - ft), axis=1))
            p = jnp.concatenate(full, axis=0)
            vaug = jnp.concatenate([window(v_ref, bi, k_start), ones], axis=1)
            o = jnp.dot(p, vaug, preferred_element_type=F32)
            o = o[:, :LANES] * (1.0 / o[:, LANES:])
            out = jnp.concatenate([jnp.where(head0, o[lo:lo + n], o[lo + n:lo + 2 * n])
                                   for lo, _, n, _ in slabs], axis=0)
            o_ref[bi, 0, pl.ds(pl.multiple_of(r * GRID_W, GRID_W), GRID_W), :] = out.astype(BF16)

    probabilities(0, pa_ref)

    def body(it, carry):
        g = 2 * it + 1
        probabilities(g, pb_ref)
        weighted_values(g - 1, pa_ref)
        probabilities(g + 1, pa_ref)
        weighted_values(g, pb_ref)
        return carry

    lax.fori_loop(0, n_groups // 2 - 1, body, 0)
    probabilities(n_groups - 1, pb_ref)
    weighted_values(n_groups - 2, pa_ref)
    weighted_values(n_groups - 1, pb_ref)


def _attn_bias(rpb):
    n_dr = 2 * WIN_ROWS - 1
    edge = GRID_W - WIN_COLS
    rpb = rpb.astype(F32)
    ext = jnp.concatenate([jnp.repeat(rpb[..., :1], edge, axis=-1), rpb,
                           jnp.repeat(rpb[..., -1:], edge + 1, axis=-1)], axis=-1)
    width = 2 * GRID_W
    flat = jnp.tile(ext, (1, 1, GRID_W))
    flat = flat[..., GRID_W - 1:GRID_W - 1 + GRID_W * (width - 1)]
    toe = flat.reshape(N_HEADS, n_dr, GRID_W, width - 1)[..., :GRID_W]
    qc = np.arange(GRID_W)[:, None]
    kc = np.arange(GRID_W)[None, :]
    ws = np.clip(qc - WIN_COLS // 2, 0, GRID_W - WIN_COLS)
    valid = (kc >= ws) & (kc < ws + WIN_COLS)
    toe = jnp.where(jnp.asarray(valid)[None, None], toe, NEG_INF)
    toe = jnp.pad(toe, ((0, 0), (0, REL_LANES // KEY_TILE_COLS - n_dr), (0, 0), (0, 0)))
    slabs = []
    for q_lo, n, ft in Q_SLABS:
        c_lo = ft * KEY_TILE_COLS
        part = toe[:, :, q_lo:q_lo + n, c_lo:c_lo + 2 * KEY_TILE_COLS]
        part = part.reshape(N_HEAD_BLOCKS, HEADS_PER_BLOCK, REL_LANES // KEY_TILE_COLS, n, 2, KEY_TILE_COLS)
        part = jnp.transpose(part, (0, 1, 3, 4, 2, 5))
        slabs.append(part.reshape(N_HEAD_BLOCKS, HEADS_PER_BLOCK * n, 2 * REL_LANES))
    return jnp.concatenate(slabs, axis=1)


def _attn_layer(q, kv, bias):
    b, _, seq2, _ = q.shape
    seq = seq2 // HEADS_PER_BLOCK
    rows = seq // GRID_W
    nb = min(ATTN_TOKENS_PER_STEP // seq, b)
    assert b % nb == 0
    kv_blk = (nb, 1, N_KEY_TILES, seq // N_KEY_TILES, LANES)
    p_scratch = pltpu.VMEM((ATTN_ROWS_PER_ITER, QBLK_ROWS, SLAB_KEYS), BF16)
    bias_scratch = pltpu.VMEM((WIN_ROWS, QBLK_ROWS, SLAB_KEYS), F32)
    return pl.pallas_call(
        functools.partial(_attn_kernel, rows=rows, nb=nb),
        grid=(N_HEAD_BLOCKS, b // nb),
        scratch_shapes=[bias_scratch, p_scratch, p_scratch],
        in_specs=[
            pl.BlockSpec((nb, 1, seq2, LANES), lambda hb, i: (i, hb, 0, 0)),
            pl.BlockSpec(kv_blk, lambda hb, i: (i, hb, 0, 0, 0)),
            pl.BlockSpec(kv_blk, lambda hb, i: (i, N_HEAD_BLOCKS + hb, 0, 0, 0)),
            pl.BlockSpec((1, QBLK_ROWS, 2 * REL_LANES), lambda hb, i: (hb, 0, 0)),
        ],
        out_specs=pl.BlockSpec((nb, 1, seq, LANES), lambda hb, i: (i, hb, 0, 0)),
        out_shape=jax.ShapeDtypeStruct((b, N_HEAD_BLOCKS, seq, LANES), BF16),
        compiler_params=_params(2),
        name="nbr_attention",
    )(q, kv, kv, bias)


def _mlp(x1, gm_ref, wup_ref, wdn_ref, gf_ref):
    h = _rms(x1, gm_ref[...]).astype(BF16)
    acc = x1
    for c in range(D_FF // FF_CHUNK):
        ff = slice(c * FF_CHUNK, (c + 1) * FF_CHUNK)
        up = jnp.dot(h, wup_ref[:, ff], preferred_element_type=F32)
        act = jnp.square(jnp.maximum(up, 0.0)).astype(BF16)
        acc = acc + jnp.dot(act, wdn_ref[ff, :], preferred_element_type=F32)
    return acc if gf_ref is None else _rms(acc, gf_ref[...])


def _attn_mlp_kernel(x_ref, a_ref, wo_ref, gm_ref, wup_ref, wdn_ref, *rest):
    *gf_ref, o_ref = rest
    attn = jnp.concatenate([a_ref[0, j] for j in range(N_HEAD_BLOCKS)], axis=-1)
    x1 = x_ref[0] + jnp.dot(attn, wo_ref[...], preferred_element_type=F32)
    o_ref[0] = _mlp(x1, gm_ref, wup_ref, wdn_ref, gf_ref[0] if gf_ref else None)


def _pool_mlp_kernel(x_ref, xp_ref, xn_ref, gx_ref, pw_ref, psc_ref, gm_ref, wup_ref, wdn_ref, *rest, sub, seq):
    *gf_ref, o_ref, ext_ref = rest
    tm = x_ref.shape[1]
    s = pl.program_id(1)
    g = gx_ref[...]
    ext_ref[0:POOL_HALO, :] = jnp.where(s > 0, _rms(xp_ref[0], g), 0.0)
    ext_ref[POOL_HALO + tm:, :] = jnp.where(s < pl.num_programs(1) - 1, _rms(xn_ref[0], g), 0.0)
    subs = [slice(k * sub, (k + 1) * sub) for k in range(tm // sub)]
    for rows in subs:
        ext_ref[POOL_HALO + rows.start:POOL_HALO + rows.stop, :] = _rms(x_ref[0, rows], g)
    for rows in subs:
        x1 = _pool_mix(ext_ref, x_ref[0, rows], rows.start, s * tm + rows.start, pw_ref, psc_ref, seq)
        o_ref[0, rows] = _mlp(x1, gm_ref, wup_ref, wdn_ref, gf_ref[0] if gf_ref else None)


def _mlp_weight_specs(d, gf):
    specs = [_const_spec((1, d)), _const_spec((d, D_FF)), _const_spec((D_FF, d))]
    return specs + ([_const_spec((1, d))] if gf is not None else [])


def _attn_mlp_layer(x, attn, wo, gm, wup, wdn, gf=None):
    b, seq, d = x.shape
    tm = TOKEN_TILE
    args = [x, attn, wo, gm, wup, wdn] + ([gf] if gf is not None else [])
    return pl.pallas_call(
        _attn_mlp_kernel,
        grid=(b, seq // tm),
        in_specs=[
            pl.BlockSpec((1, tm, d), lambda i, j: (i, j, 0)),
            pl.BlockSpec((1, N_HEAD_BLOCKS, tm, LANES), lambda i, j: (i, 0, j, 0)),
            _const_spec((d, d)),
        ] + _mlp_weight_specs(d, gf),
        out_specs=pl.BlockSpec((1, tm, d), lambda i, j: (i, j, 0)),
        out_shape=jax.ShapeDtypeStruct(x.shape, F32),
        compiler_params=_params(2),
        name="attn_mlp_block",
    )(*args)


def _pool_mlp_layer(x, gx, pw, psc, gm, wup, wdn, gf=None):
    b, seq, d = x.shape
    tm = POOL_MLP_TILE
    halo_blocks = tm // POOL_HALO
    n_halo = seq // POOL_HALO
    args = [x, x, x, gx, pw, psc, gm, wup, wdn] + ([gf] if gf is not None else [])
    return pl.pallas_call(
        functools.partial(_pool_mlp_kernel, sub=TOKEN_TILE, seq=seq),
        grid=(b, seq // tm),
        in_specs=[
            pl.BlockSpec((1, tm, d), lambda i, j: (i, j, 0)),
            pl.BlockSpec((1, POOL_HALO, d),
                         lambda i, j: (i, jnp.maximum(j * halo_blocks - 1, 0), 0)),
            pl.BlockSpec((1, POOL_HALO, d),
                         lambda i, j: (i, jnp.minimum((j + 1) * halo_blocks, n_halo - 1), 0)),
            _const_spec((1, d)),
            _const_spec((N_POOL_GROUPS, POOL_GROUP, POOL_GROUP)),
            _const_spec((1, d)),
        ] + _mlp_weight_specs(d, gf),
        out_specs=pl.BlockSpec((1, tm, d), lambda i, j: (i, j, 0)),
        out_shape=jax.ShapeDtypeStruct(x.shape, F32),
        scratch_shapes=[pltpu.VMEM((tm + 2 * POOL_HALO, d), F32)],
        compiler_params=_params(2),
        name="pool_mlp_block",
    )(*args)


def _trunk(x, p):
    depth = p["norm_mix"].shape[0]
    for i in range(depth):
        j = i // 2
        gx = p["norm_mix"][i][None]
        mlp = (p["norm_mlp"][i][None], p["w_up"][i], p["w_down"][i])
        gf = p["norm_final"][None] if i == depth - 1 else None
        if i % 2 == 0:
            x = _pool_mlp_layer(x, gx, p["pool_w"][j], p["pool_scale"][j][None], *mlp, gf=gf)
        else:
            q, kv = _qkv_layer(x, gx, p["w_qkv"][j])
            attn = _attn_layer(q, kv, p["attn_bias"][j])
            x = _attn_mlp_layer(x, attn, p["w_o"][j], *mlp, gf=gf)
    return x


def kernel(x_prompt, x_sample, norm_mix, pool_w, pool_scale, w_qkv, rpb, w_o, norm_mlp, w_up, w_down, norm_final):
    p = {
        "norm_mix": norm_mix.astype(F32),
        "norm_mlp": norm_mlp.astype(F32),
        "norm_final": norm_final.astype(F32),
        "pool_scale": pool_scale.astype(F32),
        "pool_w": pool_w.astype(BF16),
        "w_qkv": w_qkv.astype(BF16),
        "w_o": w_o.astype(BF16),
        "w_up": w_up.astype(BF16),
        "w_down": w_down.astype(BF16),
        "attn_bias": jnp.stack([_attn_bias(rpb[j]) for j in range(rpb.shape[0])]),
    }
    return (_trunk(x_prompt, p), _trunk(x_sample, p))
```

```python
import functools

import numpy as np
import jax
import jax.numpy as jnp
from jax import lax
from jax.experimental import pallas as pl
from jax.experimental.pallas import tpu as pltpu

D_MODEL = 1024
D_FF = 4 * D_MODEL
GRID_W = 64
POOL_WINDOWS = (2, 4, 8, 16)
N_POOL_GROUPS = len(POOL_WINDOWS)
POOL_GROUP = D_MODEL // N_POOL_GROUPS
POOL_HALO = 8
N_HEADS = 16
HEAD_DIM = D_MODEL // N_HEADS
HEADS_PER_BLOCK = 2
N_HEAD_BLOCKS = N_HEADS // HEADS_PER_BLOCK
LANES = HEADS_PER_BLOCK * HEAD_DIM
QBLK_ROWS = HEADS_PER_BLOCK * GRID_W
WIN_ROWS = 8
WIN_COLS = 16
WIN_KEYS = WIN_ROWS * GRID_W
KEY_TILE_COLS = 16
N_KEY_TILES = GRID_W // KEY_TILE_COLS
TILE_KEYS = WIN_ROWS * KEY_TILE_COLS
SLAB_KEYS = 2 * TILE_KEYS
REL_LANES = 2 * TILE_KEYS
Q_SLABS = ((0, 24, 0), (24, 16, 1), (40, 24, 2))
for _q_lo, _n, _ft in Q_SLABS:
    _ws = np.clip(np.arange(_q_lo, _q_lo + _n) - WIN_COLS // 2, 0, GRID_W - WIN_COLS)
    assert _ws.min() >= _ft * KEY_TILE_COLS and _ws.max() + WIN_COLS <= (_ft + 2) * KEY_TILE_COLS
RMS_EPS = 1e-6
NEG_INF = -1e30

ATTN_ROWS_PER_ITER = 16
ATTN_TOKENS_PER_STEP = 16384
TOKEN_TILE = 512
STEP_TILE = 1024
FF_CHUNK = 1024
VMEM_LIMIT_BYTES = 56 * 1024 * 1024

BF16 = jnp.bfloat16
F32 = jnp.float32


def _rms(x, g):
    ms = jnp.mean(x * x, axis=-1, keepdims=True)
    return x * lax.rsqrt(ms + RMS_EPS) * g


def _params(n_grid):
    return pltpu.CompilerParams(
        dimension_semantics=("arbitrary",) * n_grid,
        vmem_limit_bytes=VMEM_LIMIT_BYTES,
    )


def _const_spec(shape):
    nd = len(shape)
    return pl.BlockSpec(shape, lambda *_: (0,) * nd, pipeline_mode=pl.Buffered(1))


def _pool_mix(ext_ref, x, ext_lo, t0, w_ref, sc_ref, seq):
    n = x.shape[0]
    n_ext = n + 2 * POOL_HALO
    edge_row = lax.broadcasted_iota(jnp.int32, (POOL_HALO, 1), 0)
    t_edges = (t0 + edge_row, t0 + (n - POOL_HALO) + edge_row)
    out = []
    for gi, w in enumerate(POOL_WINDOWS):
        lanes = slice(gi * POOL_GROUP, (gi + 1) * POOL_GROUP)
        acc = ext_ref[ext_lo:ext_lo + n_ext, lanes]
        span = 1
        while span < w // 2:
            acc = acc + pltpu.roll(acc, n_ext - span, axis=0)
            span *= 2
        acc = acc + pltpu.roll(acc, w // 2, axis=0)
        acc = acc[POOL_HALO:POOL_HALO + n]
        inv_edges = []
        for t in t_edges:
            lo = jnp.maximum(t - w // 2, 0)
            hi = jnp.minimum(t + w // 2 - 1, seq - 1)
            inv_edges.append(1.0 / (hi - lo + 1).astype(F32))
        mean = jnp.concatenate([acc[:POOL_HALO] * inv_edges[0],
                                acc[POOL_HALO:n - POOL_HALO] * (1.0 / w),
                                acc[n - POOL_HALO:] * inv_edges[1]], axis=0)
        h = ext_ref[ext_lo + POOL_HALO:ext_lo + POOL_HALO + n, lanes]
        pooled = (mean - h).astype(BF16)
        mixed = jnp.dot(pooled, w_ref[gi], preferred_element_type=F32)
        out.append(x[:, lanes] + mixed * sc_ref[:, lanes])
    return jnp.concatenate(out, axis=1)


def _qkv_kernel(x_ref, g_ref, w_ref, q_ref, kv_ref, *, sub):
    grid_rows = sub // GRID_W
    lane = lax.broadcasted_iota(jnp.int32, (1, D_MODEL), 1)
    head0 = (lane % LANES) < HEAD_DIM

    for k in range(x_ref.shape[1] // sub):
        h = _rms(x_ref[0, k * sub:(k + 1) * sub], g_ref[...]).astype(BF16)
        q = jnp.dot(h, w_ref[:, :D_MODEL], preferred_element_type=F32) * (HEAD_DIM ** -0.5)
        pieces = []
        for row in range(grid_rows):
            for q_lo, n, _ in Q_SLABS:
                part = q[row * GRID_W + q_lo:row * GRID_W + q_lo + n]
                pieces += [jnp.where(head0, part, 0.0), jnp.where(head0, 0.0, part)]
        q = jnp.concatenate(pieces, axis=0).astype(BF16)
        q_rows = slice(k * HEADS_PER_BLOCK * sub, (k + 1) * HEADS_PER_BLOCK * sub)
        for j in range(N_HEAD_BLOCKS):
            q_ref[0, j, q_rows] = q[:, j * LANES:(j + 1) * LANES]

        kv_rows = slice(k * sub // N_KEY_TILES, (k + 1) * sub // N_KEY_TILES)
        for part in range(2):
            cols = slice((part + 1) * D_MODEL, (part + 2) * D_MODEL)
            r = jnp.dot(h, w_ref[:, cols], preferred_element_type=F32).astype(BF16)
            for t in range(N_KEY_TILES):
                tile = jnp.concatenate(
                    [r[row * GRID_W + t * KEY_TILE_COLS:row * GRID_W + (t + 1) * KEY_TILE_COLS]
                     for row in range(grid_rows)], axis=0)
                for j in range(N_HEAD_BLOCKS):
                    kv_ref[0, part * N_HEAD_BLOCKS + j, t, kv_rows] = tile[:, j * LANES:(j + 1) * LANES]


def _qkv_layer(x, g, w):
    b, seq, d = x.shape
    tm = STEP_TILE
    return pl.pallas_call(
        functools.partial(_qkv_kernel, sub=TOKEN_TILE),
        grid=(b, seq // tm),
        in_specs=[
            pl.BlockSpec((1, tm, d), lambda i, j: (i, j, 0)),
            _const_spec((1, d)),
            _const_spec((d, 3 * d)),
        ],
        out_specs=[
            pl.BlockSpec((1, N_HEAD_BLOCKS, HEADS_PER_BLOCK * tm, LANES), lambda i, j: (i, 0, j, 0)),
            pl.BlockSpec((1, 2 * N_HEAD_BLOCKS, N_KEY_TILES, tm // N_KEY_TILES, LANES),
                         lambda i, j: (i, 0, 0, j, 0)),
        ],
        out_shape=[
            jax.ShapeDtypeStruct((b, N_HEAD_BLOCKS, HEADS_PER_BLOCK * seq, LANES), BF16),
            jax.ShapeDtypeStruct((b, 2 * N_HEAD_BLOCKS, N_KEY_TILES, seq // N_KEY_TILES, LANES), BF16),
        ],
        compiler_params=_params(2),
        name="qkv_proj",
    )(x, g, w)


def _slab_rows():
    out, lo = [], 0
    for _, n, ft in Q_SLABS:
        out.append((lo, HEADS_PER_BLOCK * n, n, ft))
        lo += HEADS_PER_BLOCK * n
    return out


def _attn_kernel(q_ref, k_ref, v_ref, rel_ref, o_ref, bias_ref, pa_ref, pb_ref, *, rows, nb):
    groups_per_seq = rows // ATTN_ROWS_PER_ITER
    n_groups = nb * groups_per_seq
    assert n_groups % 2 == 0
    slabs = _slab_rows()
    ones = jnp.ones((WIN_KEYS, LANES), BF16)

    @pl.when(pl.program_id(1) == 0)
    def _():
        for tile in range(2):
            rel = rel_ref[0, :, tile * REL_LANES:(tile + 1) * REL_LANES]
            for v in range(WIN_ROWS):
                shift = (WIN_ROWS - 1 - v) * KEY_TILE_COLS
                moved = pltpu.roll(rel, REL_LANES - shift, axis=1) if shift else rel
                bias_ref[v, :, tile * TILE_KEYS:(tile + 1) * TILE_KEYS] = moved[:, :TILE_KEYS]

    def locate(g, u):
        bi = g // groups_per_seq
        r = (g % groups_per_seq) * ATTN_ROWS_PER_ITER + u
        r0 = jnp.clip(r - WIN_ROWS // 2, 0, rows - WIN_ROWS)
        return bi, r, r - r0, pl.multiple_of(r0 * KEY_TILE_COLS, KEY_TILE_COLS)

    def window(ref, bi, k_start):
        return jnp.concatenate([ref[bi, 0, t, pl.ds(k_start, TILE_KEYS), :] for t in range(N_KEY_TILES)],
                               axis=0)

    def probabilities(g, p_ref):
        where = [locate(g, u) for u in range(ATTN_ROWS_PER_ITER)]
        scores = []
        for bi, r, _, k_start in where:
            qblk = q_ref[bi, 0, pl.ds(pl.multiple_of(r * QBLK_ROWS, QBLK_ROWS), QBLK_ROWS), :]
            scores.append(lax.dot_general(qblk, window(k_ref, bi, k_start), (((1,), (1,)), ((), ())),
                                          preferred_element_type=F32))
        for u, (sc, (_, _, variant, _)) in enumerate(zip(scores, where)):
            sc = jnp.concatenate([sc[lo:lo + n2, ft * TILE_KEYS:ft * TILE_KEYS + SLAB_KEYS]
                                  for lo, n2, _, ft in slabs], axis=0)
            sc = sc + bias_ref[variant]
            m = jnp.max(sc, axis=-1, keepdims=True)
            p_ref[u] = jnp.exp(sc - m).astype(BF16)

    def weighted_values(g, p_ref):
        head0 = lax.broadcasted_iota(jnp.int32, (1, LANES), 1) < HEAD_DIM
        for u in range(ATTN_ROWS_PER_ITER):
            bi, r, _, k_start = locate(g, u)
            p = p_ref[u]
            full = []
            for lo, n2, _, ft in slabs:
                zero = jnp.zeros((n2, TILE_KEYS), BF16)
                full.append(jnp.concatenate(
                    [zero] * ft + [p[lo:lo + n2, :TILE_KEYS], p[lo:lo + n2, TILE_KEYS:]]
                    + [zero] * (N_KEY_TILES - 2 - ft), axis=1))
            p = jnp.concatenate(full, axis=0)
            vaug = jnp.concatenate([window(v_ref, bi, k_start), ones], axis=1)
            o = jnp.dot(p, vaug, preferred_element_type=F32)
            o = o[:, :LANES] * (1.0 / o[:, LANES:])
            out = jnp.concatenate([jnp.where(head0, o[lo:lo + n], o[lo + n:lo + 2 * n])
                                   for lo, _, n, _ in slabs], axis=0)
            o_ref[bi, 0, pl.ds(pl.multiple_of(r * GRID_W, GRID_W), GRID_W), :] = out.astype(BF16)

    probabilities(0, pa_ref)

    def body(it, carry):
        g = 2 * it + 1
        probabilities(g, pb_ref)
        weighted_values(g - 1, pa_ref)
        probabilities(g + 1, pa_ref)
        weighted_values(g, pb_ref)
        return carry

    lax.fori_loop(0, n_groups // 2 - 1, body, 0)
    probabilities(n_groups - 1, pb_ref)
    weighted_values(n_groups - 2, pa_ref)
    weighted_values(n_groups - 1, pb_ref)


def _attn_bias(rpb):
    n_layers = rpb.shape[0]
    n_dr = 2 * WIN_ROWS - 1
    edge = GRID_W - WIN_COLS
    rpb = rpb.astype(F32).reshape(n_layers * N_HEADS, n_dr, 2 * WIN_COLS - 1)
    ext = jnp.concatenate([jnp.repeat(rpb[..., :1], edge, axis=-1), rpb,
                           jnp.repeat(rpb[..., -1:], edge + 1, axis=-1)], axis=-1)
    width = 2 * GRID_W
    flat = jnp.tile(ext, (1, 1, GRID_W))
    flat = flat[..., GRID_W - 1:GRID_W - 1 + GRID_W * (width - 1)]
    toe = flat.reshape(n_layers * N_HEADS, n_dr, GRID_W, width - 1)[..., :GRID_W]
    qc = np.arange(GRID_W)[:, None]
    kc = np.arange(GRID_W)[None, :]
    ws = np.clip(qc - WIN_COLS // 2, 0, GRID_W - WIN_COLS)
    valid = (kc >= ws) & (kc < ws + WIN_COLS)
    toe = jnp.where(jnp.asarray(valid)[None, None], toe, NEG_INF)
    toe = jnp.pad(toe, ((0, 0), (0, REL_LANES // KEY_TILE_COLS - n_dr), (0, 0), (0, 0)))
    n_blocks = n_layers * N_HEAD_BLOCKS
    slabs = []
    for q_lo, n, ft in Q_SLABS:
        c_lo = ft * KEY_TILE_COLS
        part = toe[:, :, q_lo:q_lo + n, c_lo:c_lo + 2 * KEY_TILE_COLS]
        part = part.reshape(n_blocks, HEADS_PER_BLOCK, REL_LANES // KEY_TILE_COLS, n, 2, KEY_TILE_COLS)
        part = jnp.transpose(part, (0, 1, 3, 4, 2, 5))
        slabs.append(part.reshape(n_blocks, HEADS_PER_BLOCK * n, 2 * REL_LANES))
    return jnp.concatenate(slabs, axis=1).reshape(n_layers, N_HEAD_BLOCKS, QBLK_ROWS, 2 * REL_LANES)


def _attn_layer(q, kv, bias):
    b, _, seq2, _ = q.shape
    seq = seq2 // HEADS_PER_BLOCK
    rows = seq // GRID_W
    nb = min(ATTN_TOKENS_PER_STEP // seq, b)
    assert b % nb == 0
    kv_blk = (nb, 1, N_KEY_TILES, seq // N_KEY_TILES, LANES)
    p_scratch = pltpu.VMEM((ATTN_ROWS_PER_ITER, QBLK_ROWS, SLAB_KEYS), BF16)
    bias_scratch = pltpu.VMEM((WIN_ROWS, QBLK_ROWS, SLAB_KEYS), F32)
    return pl.pallas_call(
        functools.partial(_attn_kernel, rows=rows, nb=nb),
        grid=(N_HEAD_BLOCKS, b // nb),
        scratch_shapes=[bias_scratch, p_scratch, p_scratch],
        in_specs=[
            pl.BlockSpec((nb, 1, seq2, LANES), lambda hb, i: (i, hb, 0, 0)),
            pl.BlockSpec(kv_blk, lambda hb, i: (i, hb, 0, 0, 0)),
            pl.BlockSpec(kv_blk, lambda hb, i: (i, N_HEAD_BLOCKS + hb, 0, 0, 0)),
            pl.BlockSpec((1, QBLK_ROWS, 2 * REL_LANES), lambda hb, i: (hb, 0, 0)),
        ],
        out_specs=pl.BlockSpec((nb, 1, seq, LANES), lambda hb, i: (i, hb, 0, 0)),
        out_shape=jax.ShapeDtypeStruct((b, N_HEAD_BLOCKS, seq, LANES), BF16),
        compiler_params=_params(2),
        name="nbr_attention",
    )(q, kv, kv, bias)


def _mlp(x1, gm_ref, wup_ref, wdn_ref, gf_ref):
    h = _rms(x1, gm_ref[...]).astype(BF16)
    acc = x1
    for c in range(D_FF // FF_CHUNK):
        ff = slice(c * FF_CHUNK, (c + 1) * FF_CHUNK)
        up = jnp.dot(h, wup_ref[:, ff], preferred_element_type=F32)
        act = jnp.square(jnp.maximum(up, 0.0)).astype(BF16)
        acc = acc + jnp.dot(act, wdn_ref[ff, :], preferred_element_type=F32)
    return acc if gf_ref is None else _rms(acc, gf_ref[...])


def _attn_mlp_kernel(x_ref, a_ref, wo_ref, gm_ref, wup_ref, wdn_ref, *rest, sub):
    *gf_ref, o_ref = rest
    for k in range(x_ref.shape[1] // sub):
        rows = slice(k * sub, (k + 1) * sub)
        attn = jnp.concatenate([a_ref[0, j, rows] for j in range(N_HEAD_BLOCKS)], axis=-1)
        x1 = x_ref[0, rows] + jnp.dot(attn, wo_ref[...], preferred_element_type=F32)
        o_ref[0, rows] = _mlp(x1, gm_ref, wup_ref, wdn_ref, gf_ref[0] if gf_ref else None)


def _pool_mlp_kernel(x_ref, xp_ref, xn_ref, gx_ref, pw_ref, psc_ref, gm_ref, wup_ref, wdn_ref, *rest, sub, seq):
    *gf_ref, o_ref, ext_ref = rest
    tm = x_ref.shape[1]
    s = pl.program_id(1)
    g = gx_ref[...]
    ext_ref[0:POOL_HALO, :] = jnp.where(s > 0, _rms(xp_ref[0], g), 0.0)
    ext_ref[POOL_HALO + tm:, :] = jnp.where(s < pl.num_programs(1) - 1, _rms(xn_ref[0], g), 0.0)
    subs = [slice(k * sub, (k + 1) * sub) for k in range(tm // sub)]
    for rows in subs:
        ext_ref[POOL_HALO + rows.start:POOL_HALO + rows.stop, :] = _rms(x_ref[0, rows], g)
    for rows in subs:
        x1 = _pool_mix(ext_ref, x_ref[0, rows], rows.start, s * tm + rows.start, pw_ref, psc_ref, seq)
        o_ref[0, rows] = _mlp(x1, gm_ref, wup_ref, wdn_ref, gf_ref[0] if gf_ref else None)


def _mlp_weight_specs(d, gf):
    specs = [_const_spec((1, d)), _const_spec((d, D_FF)), _const_spec((D_FF, d))]
    return specs + ([_const_spec((1, d))] if gf is not None else [])


def _attn_mlp_layer(x, attn, wo, gm, wup, wdn, gf=None):
    b, seq, d = x.shape
    tm = STEP_TILE
    args = [x, attn, wo, gm, wup, wdn] + ([gf] if gf is not None else [])
    return pl.pallas_call(
        functools.partial(_attn_mlp_kernel, sub=TOKEN_TILE),
        grid=(b, seq // tm),
        in_specs=[
            pl.BlockSpec((1, tm, d), lambda i, j: (i, j, 0)),
            pl.BlockSpec((1, N_HEAD_BLOCKS, tm, LANES), lambda i, j: (i, 0, j, 0)),
            _const_spec((d, d)),
        ] + _mlp_weight_specs(d, gf),
        out_specs=pl.BlockSpec((1, tm, d), lambda i, j: (i, j, 0)),
        out_shape=jax.ShapeDtypeStruct(x.shape, F32),
        compiler_params=_params(2),
        name="attn_mlp_block",
    )(*args)


def _pool_mlp_layer(x, gx, pw, psc, gm, wup, wdn, gf=None):
    b, seq, d = x.shape
    tm = STEP_TILE
    halo_blocks = tm // POOL_HALO
    n_halo = seq // POOL_HALO
    args = [x, x, x, gx, pw, psc, gm, wup, wdn] + ([gf] if gf is not None else [])
    return pl.pallas_call(
        functools.partial(_pool_mlp_kernel, sub=TOKEN_TILE, seq=seq),
        grid=(b, seq // tm),
        in_specs=[
            pl.BlockSpec((1, tm, d), lambda i, j: (i, j, 0)),
            pl.BlockSpec((1, POOL_HALO, d),
                         lambda i, j: (i, jnp.maximum(j * halo_blocks - 1, 0), 0)),
            pl.BlockSpec((1, POOL_HALO, d),
                         lambda i, j: (i, jnp.minimum((j + 1) * halo_blocks, n_halo - 1), 0)),
            _const_spec((1, d)),
            _const_spec((N_POOL_GROUPS, POOL_GROUP, POOL_GROUP)),
            _const_spec((1, d)),
        ] + _mlp_weight_specs(d, gf),
        out_specs=pl.BlockSpec((1, tm, d), lambda i, j: (i, j, 0)),
        out_shape=jax.ShapeDtypeStruct(x.shape, F32),
        scratch_shapes=[pltpu.VMEM((tm + 2 * POOL_HALO, d), F32)],
        compiler_params=_params(2),
        name="pool_mlp_block",
    )(*args)


def _trunk(x, p):
    depth = p["norm_mix"].shape[0]
    for i in range(depth):
        j = i // 2
        gx = p["norm_mix"][i][None]
        mlp = (p["norm_mlp"][i][None], p["w_up"][i], p["w_down"][i])
        gf = p["norm_final"][None] if i == depth - 1 else None
        if i % 2 == 0:
            x = _pool_mlp_layer(x, gx, p["pool_w"][j], p["pool_scale"][j][None], *mlp, gf=gf)
        else:
            q, kv = _qkv_layer(x, gx, p["w_qkv"][j])
            attn = _attn_layer(q, kv, p["attn_bias"][j])
            x = _attn_mlp_layer(x, attn, p["w_o"][j], *mlp, gf=gf)
    return x


def kernel(x_prompt, x_sample, norm_mix, pool_w, pool_scale, w_qkv, rpb, w_o, norm_mlp, w_up, w_down, norm_final):
    p = {
        "norm_mix": norm_mix.astype(F32),
        "norm_mlp": norm_mlp.astype(F32),
        "norm_final": norm_final.astype(F32),
        "pool_scale": pool_scale.astype(F32),
        "pool_w": pool_w.astype(BF16),
        "w_qkv": w_qkv.astype(BF16),
        "w_o": w_o.astype(BF16),
        "w_up": w_up.astype(BF16),
        "w_down": w_down.astype(BF16),
        "attn_bias": _attn_bias(rpb),
    }
    return (_trunk(x_prompt, p), _trunk(x_sample, p))
```

```python
import functools

import numpy as np
import jax
import jax.numpy as jnp
from jax import lax
from jax.experimental import pallas as pl
from jax.experimental.pallas import tpu as pltpu

D_MODEL = 1024
D_FF = 4 * D_MODEL
GRID_W = 64
POOL_WINDOWS = (2, 4, 8, 16)
N_POOL_GROUPS = len(POOL_WINDOWS)
POOL_GROUP = D_MODEL // N_POOL_GROUPS
POOL_HALO = 8
N_HEADS = 16
HEAD_DIM = D_MODEL // N_HEADS
HEADS_PER_BLOCK = 2
N_HEAD_BLOCKS = N_HEADS // HEADS_PER_BLOCK
LANES = HEADS_PER_BLOCK * HEAD_DIM
QBLK_ROWS = HEADS_PER_BLOCK * GRID_W
WIN_ROWS = 8
WIN_COLS = 16
WIN_KEYS = WIN_ROWS * GRID_W
KEY_TILE_COLS = 16
N_KEY_TILES = GRID_W // KEY_TILE_COLS
TILE_KEYS = WIN_ROWS * KEY_TILE_COLS
SLAB_KEYS = 2 * TILE_KEYS
REL_LANES = 2 * TILE_KEYS
Q_SLABS = ((0, 24, 0), (24, 16, 1), (40, 24, 2))
for _q_lo, _n, _ft in Q_SLABS:
    _ws = np.clip(np.arange(_q_lo, _q_lo + _n) - WIN_COLS // 2, 0, GRID_W - WIN_COLS)
    assert _ws.min() >= _ft * KEY_TILE_COLS and _ws.max() + WIN_COLS <= (_ft + 2) * KEY_TILE_COLS
RMS_EPS = 1e-6
NEG_INF = -1e30

ATTN_ROWS_PER_ITER = 16
ATTN_TOKENS_PER_STEP = 16384
TOKEN_TILE = 512
STEP_TILE = 1024
POOL_SUB_TILE = 256
FF_CHUNK = 1024
N_FF_CHUNKS = D_FF // FF_CHUNK
VMEM_LIMIT_BYTES = 56 * 1024 * 1024

BF16 = jnp.bfloat16
F32 = jnp.float32


def _rms(x, g):
    ms = jnp.mean(x * x, axis=-1, keepdims=True)
    return x * lax.rsqrt(ms + RMS_EPS) * g


def _params(n_grid):
    return pltpu.CompilerParams(
        dimension_semantics=("arbitrary",) * n_grid,
        vmem_limit_bytes=VMEM_LIMIT_BYTES,
    )


def _const_spec(shape):
    nd = len(shape)
    return pl.BlockSpec(shape, lambda *_: (0,) * nd, pipeline_mode=pl.Buffered(1))


def _pool_group(ext_ref, x, ext_lo, t0, gi, w_ref, sc_ref, seq):
    n = x.shape[0]
    n_ext = n + 2 * POOL_HALO
    w = POOL_WINDOWS[gi]
    lanes = slice(gi * POOL_GROUP, (gi + 1) * POOL_GROUP)
    acc = ext_ref[ext_lo:ext_lo + n_ext, lanes]
    span = 1
    while span < w // 2:
        acc = acc + pltpu.roll(acc, n_ext - span, axis=0)
        span *= 2
    acc = acc + pltpu.roll(acc, w // 2, axis=0)
    acc = acc[POOL_HALO:POOL_HALO + n]
    edge_row = lax.broadcasted_iota(jnp.int32, (POOL_HALO, 1), 0)
    inv_edges = []
    for t in (t0 + edge_row, t0 + (n - POOL_HALO) + edge_row):
        lo = jnp.maximum(t - w // 2, 0)
        hi = jnp.minimum(t + w // 2 - 1, seq - 1)
        inv_edges.append(1.0 / (hi - lo + 1).astype(F32))
    mean = jnp.concatenate([acc[:POOL_HALO] * inv_edges[0],
                            acc[POOL_HALO:n - POOL_HALO] * (1.0 / w),
                            acc[n - POOL_HALO:] * inv_edges[1]], axis=0)
    h = ext_ref[ext_lo + POOL_HALO:ext_lo + POOL_HALO + n, lanes]
    pooled = (mean - h).astype(BF16)
    mixed = jnp.dot(pooled, w_ref[gi], preferred_element_type=F32)
    return x[:, lanes] + mixed * sc_ref[:, lanes]


def _qkv_kernel(x_ref, g_ref, w_ref, q_ref, kv_ref, *, sub):
    grid_rows = sub // GRID_W
    lane = lax.broadcasted_iota(jnp.int32, (1, D_MODEL), 1)
    head0 = (lane % LANES) < HEAD_DIM

    for k in range(x_ref.shape[1] // sub):
        h = _rms(x_ref[0, k * sub:(k + 1) * sub], g_ref[...]).astype(BF16)
        q = jnp.dot(h, w_ref[:, :D_MODEL], preferred_element_type=F32) * (HEAD_DIM ** -0.5)
        pieces = []
        for row in range(grid_rows):
            for q_lo, n, _ in Q_SLABS:
                part = q[row * GRID_W + q_lo:row * GRID_W + q_lo + n]
                pieces += [jnp.where(head0, part, 0.0), jnp.where(head0, 0.0, part)]
        q = jnp.concatenate(pieces, axis=0).astype(BF16)
        q_rows = slice(k * HEADS_PER_BLOCK * sub, (k + 1) * HEADS_PER_BLOCK * sub)
        for j in range(N_HEAD_BLOCKS):
            q_ref[0, j, q_rows] = q[:, j * LANES:(j + 1) * LANES]

        kv_rows = slice(k * sub // N_KEY_TILES, (k + 1) * sub // N_KEY_TILES)
        for part in range(2):
            cols = slice((part + 1) * D_MODEL, (part + 2) * D_MODEL)
            r = jnp.dot(h, w_ref[:, cols], preferred_element_type=F32).astype(BF16)
            for t in range(N_KEY_TILES):
                tile = jnp.concatenate(
                    [r[row * GRID_W + t * KEY_TILE_COLS:row * GRID_W + (t + 1) * KEY_TILE_COLS]
                     for row in range(grid_rows)], axis=0)
                for j in range(N_HEAD_BLOCKS):
                    kv_ref[0, part * N_HEAD_BLOCKS + j, t, kv_rows] = tile[:, j * LANES:(j + 1) * LANES]


def _qkv_layer(x, g, w):
    b, seq, d = x.shape
    tm = STEP_TILE
    return pl.pallas_call(
        functools.partial(_qkv_kernel, sub=TOKEN_TILE),
        grid=(b, seq // tm),
        in_specs=[
            pl.BlockSpec((1, tm, d), lambda i, j: (i, j, 0)),
            _const_spec((1, d)),
            _const_spec((d, 3 * d)),
        ],
        out_specs=[
            pl.BlockSpec((1, N_HEAD_BLOCKS, HEADS_PER_BLOCK * tm, LANES), lambda i, j: (i, 0, j, 0)),
            pl.BlockSpec((1, 2 * N_HEAD_BLOCKS, N_KEY_TILES, tm // N_KEY_TILES, LANES),
                         lambda i, j: (i, 0, 0, j, 0)),
        ],
        out_shape=[
            jax.ShapeDtypeStruct((b, N_HEAD_BLOCKS, HEADS_PER_BLOCK * seq, LANES), BF16),
            jax.ShapeDtypeStruct((b, 2 * N_HEAD_BLOCKS, N_KEY_TILES, seq // N_KEY_TILES, LANES), BF16),
        ],
        compiler_params=_params(2),
        name="qkv_proj",
    )(x, g, w)


def _slab_rows():
    out, lo = [], 0
    for _, n, ft in Q_SLABS:
        out.append((lo, HEADS_PER_BLOCK * n, n, ft))
        lo += HEADS_PER_BLOCK * n
    return out


def _attn_kernel(q_ref, k_ref, v_ref, rel_ref, o_ref, bias_ref, pa_ref, pb_ref, *, rows, nb):
    groups_per_seq = rows // ATTN_ROWS_PER_ITER
    n_groups = nb * groups_per_seq
    assert n_groups % 2 == 0
    slabs = _slab_rows()
    ones = jnp.ones((WIN_KEYS, LANES), BF16)

    @pl.when(pl.program_id(1) == 0)
    def _():
        for tile in range(2):
            rel = rel_ref[0, :, tile * REL_LANES:(tile + 1) * REL_LANES]
            for v in range(WIN_ROWS):
                shift = (WIN_ROWS - 1 - v) * KEY_TILE_COLS
                moved = pltpu.roll(rel, REL_LANES - shift, axis=1) if shift else rel
                bias_ref[v, :, tile * TILE_KEYS:(tile + 1) * TILE_KEYS] = moved[:, :TILE_KEYS]

    def locate(g, u):
        bi = g // groups_per_seq
        r = (g % groups_per_seq) * ATTN_ROWS_PER_ITER + u
        r0 = jnp.clip(r - WIN_ROWS // 2, 0, rows - WIN_ROWS)
        return bi, r, r - r0, pl.multiple_of(r0 * KEY_TILE_COLS, KEY_TILE_COLS)

    def window(ref, bi, k_start):
        return jnp.concatenate([ref[bi, 0, t, pl.ds(k_start, TILE_KEYS), :] for t in range(N_KEY_TILES)],
                               axis=0)

    def probabilities(g, p_ref):
        where = [locate(g, u) for u in range(ATTN_ROWS_PER_ITER)]
        scores = []
        for bi, r, _, k_start in where:
            qblk = q_ref[bi, 0, pl.ds(pl.multiple_of(r * QBLK_ROWS, QBLK_ROWS), QBLK_ROWS), :]
            scores.append(lax.dot_general(qblk, window(k_ref, bi, k_start), (((1,), (1,)), ((), ())),
                                          preferred_element_type=F32))
        for u, (sc, (_, _, variant, _)) in enumerate(zip(scores, where)):
            sc = jnp.concatenate([sc[lo:lo + n2, ft * TILE_KEYS:ft * TILE_KEYS + SLAB_KEYS]
                                  for lo, n2, _, ft in slabs], axis=0)
            sc = sc + bias_ref[variant]
            m = jnp.max(sc, axis=-1, keepdims=True)
            p_ref[u] = jnp.exp(sc - m).astype(BF16)

    def weighted_values(g, p_ref):
        head0 = lax.broadcasted_iota(jnp.int32, (1, LANES), 1) < HEAD_DIM
        for u in range(ATTN_ROWS_PER_ITER):
            bi, r, _, k_start = locate(g, u)
            p = p_ref[u]
            full = []
            for lo, n2, _, ft in slabs:
                zero = jnp.zeros((n2, TILE_KEYS), BF16)
                full.append(jnp.concatenate(
                    [zero] * ft + [p[lo:lo + n2, :TILE_KEYS], p[lo:lo + n2, TILE_KEYS:]]
                    + [zero] * (N_KEY_TILES - 2 - ft), axis=1))
            p = jnp.concatenate(full, axis=0)
            vaug = jnp.concatenate([window(v_ref, bi, k_start), ones], axis=1)
            o = jnp.dot(p, vaug, preferred_element_type=F32)
            o = o[:, :LANES] * (1.0 / o[:, LANES:])
            out = jnp.concatenate([jnp.where(head0, o[lo:lo + n], o[lo + n:lo + 2 * n])
                                   for lo, _, n, _ in slabs], axis=0)
            o_ref[bi, 0, pl.ds(pl.multiple_of(r * GRID_W, GRID_W), GRID_W), :] = out.astype(BF16)

    probabilities(0, pa_ref)

    def body(it, carry):
        g = 2 * it + 1
        probabilities(g, pb_ref)
        weighted_values(g - 1, pa_ref)
        probabilities(g + 1, pa_ref)
        weighted_values(g, pb_ref)
        return carry

    lax.fori_loop(0, n_groups // 2 - 1, body, 0)
    probabilities(n_groups - 1, pb_ref)
    weighted_values(n_groups - 2, pa_ref)
    weighted_values(n_groups - 1, pb_ref)


def _attn_bias(rpb):
    n_layers = rpb.shape[0]
    n_dr = 2 * WIN_ROWS - 1
    edge = GRID_W - WIN_COLS
    rpb = rpb.astype(F32).reshape(n_layers * N_HEADS, n_dr, 2 * WIN_COLS - 1)
    ext = jnp.concatenate([jnp.repeat(rpb[..., :1], edge, axis=-1), rpb,
                           jnp.repeat(rpb[..., -1:], edge + 1, axis=-1)], axis=-1)
    width = 2 * GRID_W
    flat = jnp.tile(ext, (1, 1, GRID_W))
    flat = flat[..., GRID_W - 1:GRID_W - 1 + GRID_W * (width - 1)]
    toe = flat.reshape(n_layers * N_HEADS, n_dr, GRID_W, width - 1)[..., :GRID_W]
    qc = np.arange(GRID_W)[:, None]
    kc = np.arange(GRID_W)[None, :]
    ws = np.clip(qc - WIN_COLS // 2, 0, GRID_W - WIN_COLS)
    valid = (kc >= ws) & (kc < ws + WIN_COLS)
    toe = jnp.where(jnp.asarray(valid)[None, None], toe, NEG_INF)
    toe = jnp.pad(toe, ((0, 0), (0, REL_LANES // KEY_TILE_COLS - n_dr), (0, 0), (0, 0)))
    n_blocks = n_layers * N_HEAD_BLOCKS
    slabs = []
    for q_lo, n, ft in Q_SLABS:
        c_lo = ft * KEY_TILE_COLS
        part = toe[:, :, q_lo:q_lo + n, c_lo:c_lo + 2 * KEY_TILE_COLS]
        part = part.reshape(n_blocks, HEADS_PER_BLOCK, REL_LANES // KEY_TILE_COLS, n, 2, KEY_TILE_COLS)
        part = jnp.transpose(part, (0, 1, 3, 4, 2, 5))
        slabs.append(part.reshape(n_blocks, HEADS_PER_BLOCK * n, 2 * REL_LANES))
    return jnp.concatenate(slabs, axis=1).reshape(n_layers, N_HEAD_BLOCKS, QBLK_ROWS, 2 * REL_LANES)


def _attn_layer(q, kv, bias):
    b, _, seq2, _ = q.shape
    seq = seq2 // HEADS_PER_BLOCK
    rows = seq // GRID_W
    nb = min(ATTN_TOKENS_PER_STEP // seq, b)
    assert b % nb == 0
    kv_blk = (nb, 1, N_KEY_TILES, seq // N_KEY_TILES, LANES)
    p_scratch = pltpu.VMEM((ATTN_ROWS_PER_ITER, QBLK_ROWS, SLAB_KEYS), BF16)
    bias_scratch = pltpu.VMEM((WIN_ROWS, QBLK_ROWS, SLAB_KEYS), F32)
    return pl.pallas_call(
        functools.partial(_attn_kernel, rows=rows, nb=nb),
        grid=(N_HEAD_BLOCKS, b // nb),
        scratch_shapes=[bias_scratch, p_scratch, p_scratch],
        in_specs=[
            pl.BlockSpec((nb, 1, seq2, LANES), lambda hb, i: (i, hb, 0, 0)),
            pl.BlockSpec(kv_blk, lambda hb, i: (i, hb, 0, 0, 0)),
            pl.BlockSpec(kv_blk, lambda hb, i: (i, N_HEAD_BLOCKS + hb, 0, 0, 0)),
            pl.BlockSpec((1, QBLK_ROWS, 2 * REL_LANES), lambda hb, i: (hb, 0, 0)),
        ],
        out_specs=pl.BlockSpec((nb, 1, seq, LANES), lambda hb, i: (i, hb, 0, 0)),
        out_shape=jax.ShapeDtypeStruct((b, N_HEAD_BLOCKS, seq, LANES), BF16),
        compiler_params=_params(2),
        name="nbr_attention",
    )(q, kv, kv, bias)


def _mlp(x1, gm_ref, wup_ref, wdn_ref, gf_ref, between_chunks=None):
    h = _rms(x1, gm_ref[...]).astype(BF16)
    acc = x1
    for c in range(N_FF_CHUNKS):
        ff = slice(c * FF_CHUNK, (c + 1) * FF_CHUNK)
        up = jnp.dot(h, wup_ref[:, ff], preferred_element_type=F32)
        act = jnp.square(jnp.maximum(up, 0.0)).astype(BF16)
        acc = acc + jnp.dot(act, wdn_ref[ff, :], preferred_element_type=F32)
        if between_chunks is not None:
            between_chunks(c)
    return acc if gf_ref is None else _rms(acc, gf_ref[...])


def _attn_mlp_kernel(x_ref, a_ref, wo_ref, gm_ref, wup_ref, wdn_ref, *rest, sub):
    *gf_ref, o_ref = rest
    subs = [slice(k * sub, (k + 1) * sub) for k in range(x_ref.shape[1] // sub)]
    x1 = []
    for rows in subs:
        attn = jnp.concatenate([a_ref[0, j, rows] for j in range(N_HEAD_BLOCKS)], axis=-1)
        x1.append(x_ref[0, rows] + jnp.dot(attn, wo_ref[...], preferred_element_type=F32))
    for rows, x1_rows in zip(subs, x1):
        o_ref[0, rows] = _mlp(x1_rows, gm_ref, wup_ref, wdn_ref, gf_ref[0] if gf_ref else None)


def _pool_mlp_kernel(x_ref, xp_ref, xn_ref, gx_ref, pw_ref, psc_ref, gm_ref, wup_ref, wdn_ref, *rest, sub, seq):
    *gf_ref, o_ref, ext_ref = rest
    assert N_FF_CHUNKS == N_POOL_GROUPS
    tm = x_ref.shape[1]
    s = pl.program_id(1)
    g = gx_ref[...]
    ext_ref[0:POOL_HALO, :] = jnp.where(s > 0, _rms(xp_ref[0], g), 0.0)
    ext_ref[POOL_HALO + tm:, :] = jnp.where(s < pl.num_programs(1) - 1, _rms(xn_ref[0], g), 0.0)
    subs = [slice(k * sub, (k + 1) * sub) for k in range(tm // sub)]
    for rows in subs:
        ext_ref[POOL_HALO + rows.start:POOL_HALO + rows.stop, :] = _rms(x_ref[0, rows], g)

    def pool_group(rows, gi):
        return _pool_group(ext_ref, x_ref[0, rows], rows.start, s * tm + rows.start, gi, pw_ref, psc_ref, seq)

    groups = [pool_group(subs[0], gi) for gi in range(N_POOL_GROUPS)]
    for k, rows in enumerate(subs):
        x1 = jnp.concatenate(groups, axis=1)
        groups = []
        mix_next = None
        if k + 1 < len(subs):
            mix_next = lambda gi, nxt=subs[k + 1]: groups.append(pool_group(nxt, gi))
        o_ref[0, rows] = _mlp(x1, gm_ref, wup_ref, wdn_ref, gf_ref[0] if gf_ref else None, mix_next)


def _mlp_weight_specs(d, gf):
    specs = [_const_spec((1, d)), _const_spec((d, D_FF)), _const_spec((D_FF, d))]
    return specs + ([_const_spec((1, d))] if gf is not None else [])


def _attn_mlp_layer(x, attn, wo, gm, wup, wdn, gf=None):
    b, seq, d = x.shape
    tm = STEP_TILE
    args = [x, attn, wo, gm, wup, wdn] + ([gf] if gf is not None else [])
    return pl.pallas_call(
        functools.partial(_attn_mlp_kernel, sub=TOKEN_TILE),
        grid=(b, seq // tm),
        in_specs=[
            pl.BlockSpec((1, tm, d), lambda i, j: (i, j, 0)),
            pl.BlockSpec((1, N_HEAD_BLOCKS, tm, LANES), lambda i, j: (i, 0, j, 0)),
            _const_spec((d, d)),
        ] + _mlp_weight_specs(d, gf),
        out_specs=pl.BlockSpec((1, tm, d), lambda i, j: (i, j, 0)),
        out_shape=jax.ShapeDtypeStruct(x.shape, F32),
        compiler_params=_params(2),
        name="attn_mlp_block",
    )(*args)


def _pool_mlp_layer(x, gx, pw, psc, gm, wup, wdn, gf=None):
    b, seq, d = x.shape
    tm = STEP_TILE
    halo_blocks = tm // POOL_HALO
    n_halo = seq // POOL_HALO
    args = [x, x, x, gx, pw, psc, gm, wup, wdn] + ([gf] if gf is not None else [])
    return pl.pallas_call(
        functools.partial(_pool_mlp_kernel, sub=POOL_SUB_TILE, seq=seq),
        grid=(b, seq // tm),
        in_specs=[
            pl.BlockSpec((1, tm, d), lambda i, j: (i, j, 0)),
            pl.BlockSpec((1, POOL_HALO, d),
                         lambda i, j: (i, jnp.maximum(j * halo_blocks - 1, 0), 0)),
            pl.BlockSpec((1, POOL_HALO, d),
                         lambda i, j: (i, jnp.minimum((j + 1) * halo_blocks, n_halo - 1), 0)),
            _const_spec((1, d)),
            _const_spec((N_POOL_GROUPS, POOL_GROUP, POOL_GROUP)),
            _const_spec((1, d)),
        ] + _mlp_weight_specs(d, gf),
        out_specs=pl.BlockSpec((1, tm, d), lambda i, j: (i, j, 0)),
        out_shape=jax.ShapeDtypeStruct(x.shape, F32),
        scratch_shapes=[pltpu.VMEM((tm + 2 * POOL_HALO, d), F32)],
        compiler_params=_params(2),
        name="pool_mlp_block",
    )(*args)


def _trunk(x, p):
    depth = p["norm_mix"].shape[0]
    for i in range(depth):
        j = i // 2
        gx = p["norm_mix"][i][None]
        mlp = (p["norm_mlp"][i][None], p["w_up"][i], p["w_down"][i])
        gf = p["norm_final"][None] if i == depth - 1 else None
        if i % 2 == 0:
            x = _pool_mlp_layer(x, gx, p["pool_w"][j], p["pool_scale"][j][None], *mlp, gf=gf)
        else:
            q, kv = _qkv_layer(x, gx, p["w_qkv"][j])
            attn = _attn_layer(q, kv, p["attn_bias"][j])
            x = _attn_mlp_layer(x, attn, p["w_o"][j], *mlp, gf=gf)
    return x


def kernel(x_prompt, x_sample, norm_mix, pool_w, pool_scale, w_qkv, rpb, w_o, norm_mlp, w_up, w_down, norm_final):
    p = {
        "norm_mix": norm_mix.astype(F32),
        "norm_mlp": norm_mlp.astype(F32),
        "norm_final": norm_final.astype(F32),
        "pool_scale": pool_scale.astype(F32),
        "pool_w": pool_w.astype(BF16),
        "w_qkv": w_qkv.astype(BF16),
        "w_o": w_o.astype(BF16),
        "w_up": w_up.astype(BF16),
        "w_down": w_down.astype(BF16),
        "attn_bias": _attn_bias(rpb),
    }
    return (_trunk(x_prompt, p), _trunk(x_sample, p))
```

```python
import functools

import numpy as np
import jax
import jax.numpy as jnp
from jax import lax
from jax.experimental import pallas as pl
from jax.experimental.pallas import tpu as pltpu

D_MODEL = 1024
D_FF = 4 * D_MODEL
GRID_W = 64
POOL_WINDOWS = (2, 4, 8, 16)
N_POOL_GROUPS = len(POOL_WINDOWS)
POOL_GROUP = D_MODEL // N_POOL_GROUPS
POOL_HALO = 8
N_HEADS = 16
HEAD_DIM = D_MODEL // N_HEADS
HEADS_PER_BLOCK = 2
N_HEAD_BLOCKS = N_HEADS // HEADS_PER_BLOCK
LANES = HEADS_PER_BLOCK * HEAD_DIM
QBLK_ROWS = HEADS_PER_BLOCK * GRID_W
WIN_ROWS = 8
WIN_COLS = 16
WIN_KEYS = WIN_ROWS * GRID_W
KEY_TILE_COLS = 16
N_KEY_TILES = GRID_W // KEY_TILE_COLS
TILE_KEYS = WIN_ROWS * KEY_TILE_COLS
SLAB_KEYS = 2 * TILE_KEYS
REL_LANES = 2 * TILE_KEYS
Q_SLABS = ((0, 24, 0), (24, 16, 1), (40, 24, 2))
for _q_lo, _n, _ft in Q_SLABS:
    _ws = np.clip(np.arange(_q_lo, _q_lo + _n) - WIN_COLS // 2, 0, GRID_W - WIN_COLS)
    assert _ws.min() >= _ft * KEY_TILE_COLS and _ws.max() + WIN_COLS <= (_ft + 2) * KEY_TILE_COLS
RMS_EPS = 1e-6
NEG_INF = -1e30

ATTN_ROWS_PER_ITER = 16
ATTN_TOKENS_PER_STEP = 16384
TOKEN_TILE = 512
STEP_TILE = 1024
POOL_SUB_TILE = 256
FF_CHUNK = 1024
N_FF_CHUNKS = D_FF // FF_CHUNK
VMEM_LIMIT_BYTES = 56 * 1024 * 1024

BF16 = jnp.bfloat16
F32 = jnp.float32


def _rms(x, g):
    ms = jnp.mean(x * x, axis=-1, keepdims=True)
    return x * lax.rsqrt(ms + RMS_EPS) * g


def _params(n_grid):
    return pltpu.CompilerParams(
        dimension_semantics=("arbitrary",) * n_grid,
        vmem_limit_bytes=VMEM_LIMIT_BYTES,
    )


def _const_spec(shape, layer=None):
    nd = len(shape)
    if layer is None:
        return pl.BlockSpec(shape, lambda *_: (0,) * nd, pipeline_mode=pl.Buffered(1))
    return pl.BlockSpec((None,) + tuple(shape), lambda *_: (layer,) + (0,) * nd, pipeline_mode=pl.Buffered(1))


def _pool_group(ext_ref, x, ext_lo, t0, gi, w_ref, sc_ref, seq):
    n = x.shape[0]
    n_ext = n + 2 * POOL_HALO
    w = POOL_WINDOWS[gi]
    lanes = slice(gi * POOL_GROUP, (gi + 1) * POOL_GROUP)
    acc = ext_ref[ext_lo:ext_lo + n_ext, lanes]
    span = 1
    while span < w // 2:
        acc = acc + pltpu.roll(acc, n_ext - span, axis=0)
        span *= 2
    acc = acc + pltpu.roll(acc, w // 2, axis=0)
    acc = acc[POOL_HALO:POOL_HALO + n]
    edge_row = lax.broadcasted_iota(jnp.int32, (POOL_HALO, 1), 0)
    inv_edges = []
    for t in (t0 + edge_row, t0 + (n - POOL_HALO) + edge_row):
        lo = jnp.maximum(t - w // 2, 0)
        hi = jnp.minimum(t + w // 2 - 1, seq - 1)
        inv_edges.append(1.0 / (hi - lo + 1).astype(F32))
    mean = jnp.concatenate([acc[:POOL_HALO] * inv_edges[0],
                            acc[POOL_HALO:n - POOL_HALO] * (1.0 / w),
                            acc[n - POOL_HALO:] * inv_edges[1]], axis=0)
    h = ext_ref[ext_lo + POOL_HALO:ext_lo + POOL_HALO + n, lanes]
    pooled = (mean - h).astype(BF16)
    mixed = jnp.dot(pooled, w_ref[gi], preferred_element_type=F32)
    return x[:, lanes] + mixed * sc_ref[:, lanes]


def _qkv_kernel(x_ref, g_ref, w_ref, q_ref, kv_ref, *, sub):
    grid_rows = sub // GRID_W
    lane = lax.broadcasted_iota(jnp.int32, (1, D_MODEL), 1)
    head0 = (lane % LANES) < HEAD_DIM

    for k in range(x_ref.shape[1] // sub):
        h = _rms(x_ref[0, k * sub:(k + 1) * sub], g_ref[...]).astype(BF16)
        q = jnp.dot(h, w_ref[:, :D_MODEL], preferred_element_type=F32) * (HEAD_DIM ** -0.5)
        pieces = []
        for row in range(grid_rows):
            for q_lo, n, _ in Q_SLABS:
                part = q[row * GRID_W + q_lo:row * GRID_W + q_lo + n]
                pieces += [jnp.where(head0, part, 0.0), jnp.where(head0, 0.0, part)]
        q = jnp.concatenate(pieces, axis=0).astype(BF16)
        q_rows = slice(k * HEADS_PER_BLOCK * sub, (k + 1) * HEADS_PER_BLOCK * sub)
        for j in range(N_HEAD_BLOCKS):
            q_ref[0, j, q_rows] = q[:, j * LANES:(j + 1) * LANES]

        kv_rows = slice(k * sub // N_KEY_TILES, (k + 1) * sub // N_KEY_TILES)
        for part in range(2):
            cols = slice((part + 1) * D_MODEL, (part + 2) * D_MODEL)
            r = jnp.dot(h, w_ref[:, cols], preferred_element_type=F32).astype(BF16)
            for t in range(N_KEY_TILES):
                tile = jnp.concatenate(
                    [r[row * GRID_W + t * KEY_TILE_COLS:row * GRID_W + (t + 1) * KEY_TILE_COLS]
                     for row in range(grid_rows)], axis=0)
                for j in range(N_HEAD_BLOCKS):
                    kv_ref[0, part * N_HEAD_BLOCKS + j, t, kv_rows] = tile[:, j * LANES:(j + 1) * LANES]


def _qkv_layer(x, g, w, mixer):
    b, seq, d = x.shape
    tm = STEP_TILE
    return pl.pallas_call(
        functools.partial(_qkv_kernel, sub=TOKEN_TILE),
        grid=(b, seq // tm),
        in_specs=[
            pl.BlockSpec((1, tm, d), lambda i, j: (i, j, 0)),
            _const_spec((1, d)),
            _const_spec((d, 3 * d), mixer),
        ],
        out_specs=[
            pl.BlockSpec((1, N_HEAD_BLOCKS, HEADS_PER_BLOCK * tm, LANES), lambda i, j: (i, 0, j, 0)),
            pl.BlockSpec((1, 2 * N_HEAD_BLOCKS, N_KEY_TILES, tm // N_KEY_TILES, LANES),
                         lambda i, j: (i, 0, 0, j, 0)),
        ],
        out_shape=[
            jax.ShapeDtypeStruct((b, N_HEAD_BLOCKS, HEADS_PER_BLOCK * seq, LANES), BF16),
            jax.ShapeDtypeStruct((b, 2 * N_HEAD_BLOCKS, N_KEY_TILES, seq // N_KEY_TILES, LANES), BF16),
        ],
        compiler_params=_params(2),
        name="qkv_proj",
    )(x, g, w)


def _slab_rows():
    out, lo = [], 0
    for _, n, ft in Q_SLABS:
        out.append((lo, HEADS_PER_BLOCK * n, n, ft))
        lo += HEADS_PER_BLOCK * n
    return out


def _attn_kernel(q_ref, k_ref, v_ref, rel_ref, o_ref, bias_ref, pa_ref, pb_ref, *, rows, nb):
    groups_per_seq = rows // ATTN_ROWS_PER_ITER
    n_groups = nb * groups_per_seq
    assert n_groups % 2 == 0
    slabs = _slab_rows()
    ones = jnp.ones((WIN_KEYS, LANES), BF16)

    @pl.when(pl.program_id(1) == 0)
    def _():
        for tile in range(2):
            rel = rel_ref[0, :, tile * REL_LANES:(tile + 1) * REL_LANES]
            for v in range(WIN_ROWS):
                shift = (WIN_ROWS - 1 - v) * KEY_TILE_COLS
                moved = pltpu.roll(rel, REL_LANES - shift, axis=1) if shift else rel
                bias_ref[v, :, tile * TILE_KEYS:(tile + 1) * TILE_KEYS] = moved[:, :TILE_KEYS]

    def locate(g, u):
        bi = g // groups_per_seq
        r = (g % groups_per_seq) * ATTN_ROWS_PER_ITER + u
        r0 = jnp.clip(r - WIN_ROWS // 2, 0, rows - WIN_ROWS)
        return bi, r, r - r0, pl.multiple_of(r0 * KEY_TILE_COLS, KEY_TILE_COLS)

    def window(ref, bi, k_start):
        return jnp.concatenate([ref[bi, 0, t, pl.ds(k_start, TILE_KEYS), :] for t in range(N_KEY_TILES)],
                               axis=0)

    def probabilities(g, p_ref):
        where = [locate(g, u) for u in range(ATTN_ROWS_PER_ITER)]
        scores = []
        for bi, r, _, k_start in where:
            qblk = q_ref[bi, 0, pl.ds(pl.multiple_of(r * QBLK_ROWS, QBLK_ROWS), QBLK_ROWS), :]
            scores.append(lax.dot_general(qblk, window(k_ref, bi, k_start), (((1,), (1,)), ((), ())),
                                          preferred_element_type=F32))
        for u, (sc, (_, _, variant, _)) in enumerate(zip(scores, where)):
            sc = jnp.concatenate([sc[lo:lo + n2, ft * TILE_KEYS:ft * TILE_KEYS + SLAB_KEYS]
                                  for lo, n2, _, ft in slabs], axis=0)
            sc = sc + bias_ref[variant]
            m = jnp.max(sc, axis=-1, keepdims=True)
            p_ref[u] = jnp.exp(sc - m).astype(BF16)

    def weighted_values(g, p_ref):
        head0 = lax.broadcasted_iota(jnp.int32, (1, LANES), 1) < HEAD_DIM
        for u in range(ATTN_ROWS_PER_ITER):
            bi, r, _, k_start = locate(g, u)
            p = p_ref[u]
            full = []
            for lo, n2, _, ft in slabs:
                zero = jnp.zeros((n2, TILE_KEYS), BF16)
                full.append(jnp.concatenate(
                    [zero] * ft + [p[lo:lo + n2, :TILE_KEYS], p[lo:lo + n2, TILE_KEYS:]]
                    + [zero] * (N_KEY_TILES - 2 - ft), axis=1))
            p = jnp.concatenate(full, axis=0)
            vaug = jnp.concatenate([window(v_ref, bi, k_start), ones], axis=1)
            o = jnp.dot(p, vaug, preferred_element_type=F32)
            o = o[:, :LANES] * (1.0 / o[:, LANES:])
            out = jnp.concatenate([jnp.where(head0, o[lo:lo + n], o[lo + n:lo + 2 * n])
                                   for lo, _, n, _ in slabs], axis=0)
            o_ref[bi, 0, pl.ds(pl.multiple_of(r * GRID_W, GRID_W), GRID_W), :] = out.astype(BF16)

    probabilities(0, pa_ref)

    def body(it, carry):
        g = 2 * it + 1
        probabilities(g, pb_ref)
        weighted_values(g - 1, pa_ref)
        probabilities(g + 1, pa_ref)
        weighted_values(g, pb_ref)
        return carry

    lax.fori_loop(0, n_groups // 2 - 1, body, 0)
    probabilities(n_groups - 1, pb_ref)
    weighted_values(n_groups - 2, pa_ref)
    weighted_values(n_groups - 1, pb_ref)


def _attn_bias(rpb):
    n_layers = rpb.shape[0]
    n_dr = 2 * WIN_ROWS - 1
    edge = GRID_W - WIN_COLS
    rpb = rpb.astype(F32).reshape(n_layers * N_HEADS, n_dr, 2 * WIN_COLS - 1)
    ext = jnp.concatenate([jnp.repeat(rpb[..., :1], edge, axis=-1), rpb,
                           jnp.repeat(rpb[..., -1:], edge + 1, axis=-1)], axis=-1)
    width = 2 * GRID_W
    flat = jnp.tile(ext, (1, 1, GRID_W))
    flat = flat[..., GRID_W - 1:GRID_W - 1 + GRID_W * (width - 1)]
    toe = flat.reshape(n_layers * N_HEADS, n_dr, GRID_W, width - 1)[..., :GRID_W]
    qc = np.arange(GRID_W)[:, None]
    kc = np.arange(GRID_W)[None, :]
    ws = np.clip(qc - WIN_COLS // 2, 0, GRID_W - WIN_COLS)
    valid = (kc >= ws) & (kc < ws + WIN_COLS)
    toe = jnp.where(jnp.asarray(valid)[None, None], toe, NEG_INF)
    toe = jnp.pad(toe, ((0, 0), (0, REL_LANES // KEY_TILE_COLS - n_dr), (0, 0), (0, 0)))
    n_blocks = n_layers * N_HEAD_BLOCKS
    slabs = []
    for q_lo, n, ft in Q_SLABS:
        c_lo = ft * KEY_TILE_COLS
        part = toe[:, :, q_lo:q_lo + n, c_lo:c_lo + 2 * KEY_TILE_COLS]
        part = part.reshape(n_blocks, HEADS_PER_BLOCK, REL_LANES // KEY_TILE_COLS, n, 2, KEY_TILE_COLS)
        part = jnp.transpose(part, (0, 1, 3, 4, 2, 5))
        slabs.append(part.reshape(n_blocks, HEADS_PER_BLOCK * n, 2 * REL_LANES))
    return jnp.concatenate(slabs, axis=1).reshape(n_layers, N_HEAD_BLOCKS, QBLK_ROWS, 2 * REL_LANES)


def _attn_layer(q, kv, bias, mixer):
    b, _, seq2, _ = q.shape
    seq = seq2 // HEADS_PER_BLOCK
    rows = seq // GRID_W
    nb = min(ATTN_TOKENS_PER_STEP // seq, b)
    assert b % nb == 0
    kv_blk = (nb, 1, N_KEY_TILES, seq // N_KEY_TILES, LANES)
    p_scratch = pltpu.VMEM((ATTN_ROWS_PER_ITER, QBLK_ROWS, SLAB_KEYS), BF16)
    bias_scratch = pltpu.VMEM((WIN_ROWS, QBLK_ROWS, SLAB_KEYS), F32)
    return pl.pallas_call(
        functools.partial(_attn_kernel, rows=rows, nb=nb),
        grid=(N_HEAD_BLOCKS, b // nb),
        scratch_shapes=[bias_scratch, p_scratch, p_scratch],
        in_specs=[
            pl.BlockSpec((nb, 1, seq2, LANES), lambda hb, i: (i, hb, 0, 0)),
            pl.BlockSpec(kv_blk, lambda hb, i: (i, hb, 0, 0, 0)),
            pl.BlockSpec(kv_blk, lambda hb, i: (i, N_HEAD_BLOCKS + hb, 0, 0, 0)),
            pl.BlockSpec((None, 1, QBLK_ROWS, 2 * REL_LANES), lambda hb, i: (mixer, hb, 0, 0)),
        ],
        out_specs=pl.BlockSpec((nb, 1, seq, LANES), lambda hb, i: (i, hb, 0, 0)),
        out_shape=jax.ShapeDtypeStruct((b, N_HEAD_BLOCKS, seq, LANES), BF16),
        compiler_params=_params(2),
        name="nbr_attention",
    )(q, kv, kv, bias)


def _mlp(x1, gm_ref, wup_ref, wdn_ref, gf_ref, between_chunks=None):
    h = _rms(x1, gm_ref[...]).astype(BF16)
    acc = x1
    for c in range(N_FF_CHUNKS):
        ff = slice(c * FF_CHUNK, (c + 1) * FF_CHUNK)
        up = jnp.dot(h, wup_ref[:, ff], preferred_element_type=F32)
        act = jnp.square(jnp.maximum(up, 0.0)).astype(BF16)
        acc = acc + jnp.dot(act, wdn_ref[ff, :], preferred_element_type=F32)
        if between_chunks is not None:
            between_chunks(c)
    return acc if gf_ref is None else _rms(acc, gf_ref[...])


def _attn_mlp_kernel(x_ref, a_ref, wo_ref, gm_ref, wup_ref, wdn_ref, *rest, sub):
    *gf_ref, o_ref = rest
    subs = [slice(k * sub, (k + 1) * sub) for k in range(x_ref.shape[1] // sub)]
    x1 = []
    for rows in subs:
        attn = jnp.concatenate([a_ref[0, j, rows] for j in range(N_HEAD_BLOCKS)], axis=-1)
        x1.append(x_ref[0, rows] + jnp.dot(attn, wo_ref[...], preferred_element_type=F32))
    for rows, x1_rows in zip(subs, x1):
        o_ref[0, rows] = _mlp(x1_rows, gm_ref, wup_ref, wdn_ref, gf_ref[0] if gf_ref else None)


def _pool_mlp_kernel(x_ref, xp_ref, xn_ref, gx_ref, pw_ref, psc_ref, gm_ref, wup_ref, wdn_ref, *rest, sub, seq):
    *gf_ref, o_ref, ext_ref = rest
    assert N_FF_CHUNKS == N_POOL_GROUPS
    tm = x_ref.shape[1]
    s = pl.program_id(1)
    g = gx_ref[...]
    ext_ref[0:POOL_HALO, :] = jnp.where(s > 0, _rms(xp_ref[0], g), 0.0)
    ext_ref[POOL_HALO + tm:, :] = jnp.where(s < pl.num_programs(1) - 1, _rms(xn_ref[0], g), 0.0)
    subs = [slice(k * sub, (k + 1) * sub) for k in range(tm // sub)]
    for rows in subs:
        ext_ref[POOL_HALO + rows.start:POOL_HALO + rows.stop, :] = _rms(x_ref[0, rows], g)

    def pool_group(rows, gi):
        return _pool_group(ext_ref, x_ref[0, rows], rows.start, s * tm + rows.start, gi, pw_ref, psc_ref, seq)

    groups = [pool_group(subs[0], gi) for gi in range(N_POOL_GROUPS)]
    for k, rows in enumerate(subs):
        x1 = jnp.concatenate(groups, axis=1)
        groups = []
        mix_next = None
        if k + 1 < len(subs):
            mix_next = lambda gi, nxt=subs[k + 1]: groups.append(pool_group(nxt, gi))
        o_ref[0, rows] = _mlp(x1, gm_ref, wup_ref, wdn_ref, gf_ref[0] if gf_ref else None, mix_next)


def _mlp_weight_specs(d, layer, gf):
    specs = [_const_spec((1, d)), _const_spec((d, D_FF), layer), _const_spec((D_FF, d), layer)]
    return specs + ([_const_spec((1, d))] if gf is not None else [])


def _attn_mlp_layer(x, attn, wo, mixer, gm, wup, wdn, layer, gf=None):
    b, seq, d = x.shape
    tm = STEP_TILE
    args = [x, attn, wo, gm, wup, wdn] + ([gf] if gf is not None else [])
    return pl.pallas_call(
        functools.partial(_attn_mlp_kernel, sub=TOKEN_TILE),
        grid=(b, seq // tm),
        in_specs=[
            pl.BlockSpec((1, tm, d), lambda i, j: (i, j, 0)),
            pl.BlockSpec((1, N_HEAD_BLOCKS, tm, LANES), lambda i, j: (i, 0, j, 0)),
            _const_spec((d, d), mixer),
        ] + _mlp_weight_specs(d, layer, gf),
        out_specs=pl.BlockSpec((1, tm, d), lambda i, j: (i, j, 0)),
        out_shape=jax.ShapeDtypeStruct(x.shape, F32),
        compiler_params=_params(2),
        name="attn_mlp_block",
    )(*args)


def _pool_mlp_layer(x, gx, pw, mixer, psc, gm, wup, wdn, layer, gf=None):
    b, seq, d = x.shape
    tm = STEP_TILE
    halo_blocks = tm // POOL_HALO
    n_halo = seq // POOL_HALO
    args = [x, x, x, gx, pw, psc, gm, wup, wdn] + ([gf] if gf is not None else [])
    return pl.pallas_call(
        functools.partial(_pool_mlp_kernel, sub=POOL_SUB_TILE, seq=seq),
        grid=(b, seq // tm),
        in_specs=[
            pl.BlockSpec((1, tm, d), lambda i, j: (i, j, 0)),
            pl.BlockSpec((1, POOL_HALO, d),
                         lambda i, j: (i, jnp.maximum(j * halo_blocks - 1, 0), 0)),
            pl.BlockSpec((1, POOL_HALO, d),
                         lambda i, j: (i, jnp.minimum((j + 1) * halo_blocks, n_halo - 1), 0)),
            _const_spec((1, d)),
            _const_spec((N_POOL_GROUPS, POOL_GROUP, POOL_GROUP), mixer),
            _const_spec((1, d)),
        ] + _mlp_weight_specs(d, layer, gf),
        out_specs=pl.BlockSpec((1, tm, d), lambda i, j: (i, j, 0)),
        out_shape=jax.ShapeDtypeStruct(x.shape, F32),
        scratch_shapes=[pltpu.VMEM((tm + 2 * POOL_HALO, d), F32)],
        compiler_params=_params(2),
        name="pool_mlp_block",
    )(*args)


def _trunk(x, p):
    depth = p["norm_mix"].shape[0]
    for i in range(depth):
        j = i // 2
        gx = p["norm_mix"][i][None]
        mlp = (p["norm_mlp"][i][None], p["w_up"], p["w_down"], i)
        gf = p["norm_final"][None] if i == depth - 1 else None
        if i % 2 == 0:
            x = _pool_mlp_layer(x, gx, p["pool_w"], j, p["pool_scale"][j][None], *mlp, gf=gf)
        else:
            q, kv = _qkv_layer(x, gx, p["w_qkv"], j)
            attn = _attn_layer(q, kv, p["attn_bias"], j)
            x = _attn_mlp_layer(x, attn, p["w_o"], j, *mlp, gf=gf)
    return x


def kernel(x_prompt, x_sample, norm_mix, pool_w, pool_scale, w_qkv, rpb, w_o, norm_mlp, w_up, w_down, norm_final):
    p = {
        "norm_mix": norm_mix.astype(F32),
        "norm_mlp": norm_mlp.astype(F32),
        "norm_final": norm_final.astype(F32),
        "pool_scale": pool_scale.astype(F32),
        "pool_w": pool_w.astype(BF16),
        "w_qkv": w_qkv.astype(BF16),
        "w_o": w_o.astype(BF16),
        "w_up": w_up.astype(BF16),
        "w_down": w_down.astype(BF16),
        "attn_bias": _attn_bias(rpb),
    }
    return (_trunk(x_prompt, p), _trunk(x_sample, p))
```

```python
import functools

import numpy as np
import jax
import jax.numpy as jnp
from jax import lax
from jax.experimental import pallas as pl
from jax.experimental.pallas import tpu as pltpu

D_MODEL = 1024
D_FF = 4 * D_MODEL
GRID_W = 64
POOL_WINDOWS = (2, 4, 8, 16)
N_POOL_GROUPS = len(POOL_WINDOWS)
POOL_GROUP = D_MODEL // N_POOL_GROUPS
POOL_HALO = 8
N_HEADS = 16
HEAD_DIM = D_MODEL // N_HEADS
HEADS_PER_BLOCK = 2
N_HEAD_BLOCKS = N_HEADS // HEADS_PER_BLOCK
LANES = HEADS_PER_BLOCK * HEAD_DIM
QBLK_ROWS = HEADS_PER_BLOCK * GRID_W
WIN_ROWS = 8
WIN_COLS = 16
WIN_KEYS = WIN_ROWS * GRID_W
KEY_TILE_COLS = 16
N_KEY_TILES = GRID_W // KEY_TILE_COLS
TILE_KEYS = WIN_ROWS * KEY_TILE_COLS
SLAB_KEYS = 2 * TILE_KEYS
REL_LANES = 2 * TILE_KEYS
Q_SLABS = ((0, 24, 0), (24, 16, 1), (40, 24, 2))
for _q_lo, _n, _ft in Q_SLABS:
    _ws = np.clip(np.arange(_q_lo, _q_lo + _n) - WIN_COLS // 2, 0, GRID_W - WIN_COLS)
    assert _ws.min() >= _ft * KEY_TILE_COLS and _ws.max() + WIN_COLS <= (_ft + 2) * KEY_TILE_COLS
RMS_EPS = 1e-6
NEG_INF = -1e30

ATTN_ROWS_PER_ITER = 16
ATTN_TOKENS_PER_STEP = 16384
TOKEN_TILE = 512
STEP_TILE = 1024
POOL_SUB_TILE = 256
FF_CHUNK = 1024
N_FF_CHUNKS = D_FF // FF_CHUNK
VMEM_LIMIT_BYTES = 56 * 1024 * 1024

BF16 = jnp.bfloat16
F32 = jnp.float32


def _rms(x, g):
    ms = jnp.mean(x * x, axis=-1, keepdims=True)
    return x * lax.rsqrt(ms + RMS_EPS) * g


def _params(n_grid):
    return pltpu.CompilerParams(
        dimension_semantics=("arbitrary",) * n_grid,
        vmem_limit_bytes=VMEM_LIMIT_BYTES,
    )


def _const_spec(shape, layer=None):
    nd = len(shape)
    if layer is None:
        return pl.BlockSpec(shape, lambda *_: (0,) * nd, pipeline_mode=pl.Buffered(1))
    return pl.BlockSpec((None,) + tuple(shape), lambda *_: (layer,) + (0,) * nd, pipeline_mode=pl.Buffered(1))


def _pool_group(ext_ref, x, ext_lo, t0, gi, w_ref, sc_ref, seq):
    n = x.shape[0]
    n_ext = n + 2 * POOL_HALO
    w = POOL_WINDOWS[gi]
    lanes = slice(gi * POOL_GROUP, (gi + 1) * POOL_GROUP)
    acc = ext_ref[ext_lo:ext_lo + n_ext, lanes]
    span = 1
    while span < w // 2:
        acc = acc + pltpu.roll(acc, n_ext - span, axis=0)
        span *= 2
    acc = acc + pltpu.roll(acc, w // 2, axis=0)
    acc = acc[POOL_HALO:POOL_HALO + n]
    edge_row = lax.broadcasted_iota(jnp.int32, (POOL_HALO, 1), 0)
    inv_edges = []
    for t in (t0 + edge_row, t0 + (n - POOL_HALO) + edge_row):
        lo = jnp.maximum(t - w // 2, 0)
        hi = jnp.minimum(t + w // 2 - 1, seq - 1)
        inv_edges.append(1.0 / (hi - lo + 1).astype(F32))
    mean = jnp.concatenate([acc[:POOL_HALO] * inv_edges[0],
                            acc[POOL_HALO:n - POOL_HALO] * (1.0 / w),
                            acc[n - POOL_HALO:] * inv_edges[1]], axis=0)
    h = ext_ref[ext_lo + POOL_HALO:ext_lo + POOL_HALO + n, lanes]
    pooled = (mean - h).astype(BF16)
    mixed = jnp.dot(pooled, w_ref[gi], preferred_element_type=F32)
    return x[:, lanes] + mixed * sc_ref[:, lanes]


def _qkv_kernel(x_ref, g_ref, w_ref, q_ref, kv_ref, *, sub):
    grid_rows = sub // GRID_W
    lane = lax.broadcasted_iota(jnp.int32, (1, D_MODEL), 1)
    head0 = (lane % LANES) < HEAD_DIM

    for k in range(x_ref.shape[1] // sub):
        h = _rms(x_ref[0, k * sub:(k + 1) * sub], g_ref[...]).astype(BF16)
        q = jnp.dot(h, w_ref[:, :D_MODEL], preferred_element_type=F32) * (HEAD_DIM ** -0.5)
        pieces = []
        for row in range(grid_rows):
            for q_lo, n, _ in Q_SLABS:
                part = q[row * GRID_W + q_lo:row * GRID_W + q_lo + n]
                pieces += [jnp.where(head0, part, 0.0), jnp.where(head0, 0.0, part)]
        q = jnp.concatenate(pieces, axis=0).astype(BF16)
        q_rows = slice(k * HEADS_PER_BLOCK * sub, (k + 1) * HEADS_PER_BLOCK * sub)
        for j in range(N_HEAD_BLOCKS):
            q_ref[0, j, q_rows] = q[:, j * LANES:(j + 1) * LANES]

        kv_rows = slice(k * sub // N_KEY_TILES, (k + 1) * sub // N_KEY_TILES)
        for part in range(2):
            cols = slice((part + 1) * D_MODEL, (part + 2) * D_MODEL)
            r = jnp.dot(h, w_ref[:, cols], preferred_element_type=F32).astype(BF16)
            for t in range(N_KEY_TILES):
                tile = jnp.concatenate(
                    [r[row * GRID_W + t * KEY_TILE_COLS:row * GRID_W + (t + 1) * KEY_TILE_COLS]
                     for row in range(grid_rows)], axis=0)
                for j in range(N_HEAD_BLOCKS):
                    kv_ref[0, part * N_HEAD_BLOCKS + j, t, kv_rows] = tile[:, j * LANES:(j + 1) * LANES]


def _qkv_layer(x, g, w, mixer):
    b, seq, d = x.shape
    tm = STEP_TILE
    return pl.pallas_call(
        functools.partial(_qkv_kernel, sub=TOKEN_TILE),
        grid=(b, seq // tm),
        in_specs=[
            pl.BlockSpec((1, tm, d), lambda i, j: (i, j, 0)),
            _const_spec((1, d)),
            _const_spec((d, 3 * d), mixer),
        ],
        out_specs=[
            pl.BlockSpec((1, N_HEAD_BLOCKS, HEADS_PER_BLOCK * tm, LANES), lambda i, j: (i, 0, j, 0)),
            pl.BlockSpec((1, 2 * N_HEAD_BLOCKS, N_KEY_TILES, tm // N_KEY_TILES, LANES),
                         lambda i, j: (i, 0, 0, j, 0)),
        ],
        out_shape=[
            jax.ShapeDtypeStruct((b, N_HEAD_BLOCKS, HEADS_PER_BLOCK * seq, LANES), BF16),
            jax.ShapeDtypeStruct((b, 2 * N_HEAD_BLOCKS, N_KEY_TILES, seq // N_KEY_TILES, LANES), BF16),
        ],
        compiler_params=_params(2),
        name="qkv_proj",
    )(x, g, w)


def _slab_rows():
    out, lo = [], 0
    for _, n, ft in Q_SLABS:
        out.append((lo, HEADS_PER_BLOCK * n, n, ft))
        lo += HEADS_PER_BLOCK * n
    return out


def _attn_kernel(q_ref, k_ref, v_ref, rel_ref, o_ref, bias_ref, pa_ref, pb_ref, *, rows, nb):
    groups_per_seq = rows // ATTN_ROWS_PER_ITER
    n_groups = nb * groups_per_seq
    assert n_groups % 2 == 0
    slabs = _slab_rows()
    ones = jnp.ones((WIN_KEYS, LANES), BF16)

    @pl.when(pl.program_id(1) == 0)
    def _():
        for tile in range(2):
            rel = rel_ref[0, :, tile * REL_LANES:(tile + 1) * REL_LANES]
            for v in range(WIN_ROWS):
                shift = (WIN_ROWS - 1 - v) * KEY_TILE_COLS
                moved = pltpu.roll(rel, REL_LANES - shift, axis=1) if shift else rel
                bias_ref[v, :, tile * TILE_KEYS:(tile + 1) * TILE_KEYS] = moved[:, :TILE_KEYS]

    def locate(g, u):
        bi = g // groups_per_seq
        r = (g % groups_per_seq) * ATTN_ROWS_PER_ITER + u
        r0 = jnp.clip(r - WIN_ROWS // 2, 0, rows - WIN_ROWS)
        return bi, r, r - r0, pl.multiple_of(r0 * KEY_TILE_COLS, KEY_TILE_COLS)

    def window(ref, bi, k_start):
        return jnp.concatenate([ref[bi, 0, t, pl.ds(k_start, TILE_KEYS), :] for t in range(N_KEY_TILES)],
                               axis=0)

    def probabilities(g, p_ref):
        where = [locate(g, u) for u in range(ATTN_ROWS_PER_ITER)]
        scores = []
        for bi, r, _, k_start in where:
            qblk = q_ref[bi, 0, pl.ds(pl.multiple_of(r * QBLK_ROWS, QBLK_ROWS), QBLK_ROWS), :]
            scores.append(lax.dot_general(qblk, window(k_ref, bi, k_start), (((1,), (1,)), ((), ())),
                                          preferred_element_type=F32))
        for u, (sc, (_, _, variant, _)) in enumerate(zip(scores, where)):
            sc = jnp.concatenate([sc[lo:lo + n2, ft * TILE_KEYS:ft * TILE_KEYS + SLAB_KEYS]
                                  for lo, n2, _, ft in slabs], axis=0)
            sc = sc + bias_ref[variant]
            m = jnp.max(sc, axis=-1, keepdims=True)
            p_ref[u] = jnp.exp(sc - m).astype(BF16)

    def weighted_values(g, p_ref):
        head0 = lax.broadcasted_iota(jnp.int32, (1, LANES), 1) < HEAD_DIM
        for u in range(ATTN_ROWS_PER_ITER):
            bi, r, _, k_start = locate(g, u)
            p = p_ref[u]
            full = []
            for lo, n2, _, ft in slabs:
                zero = jnp.zeros((n2, TILE_KEYS), BF16)
                full.append(jnp.concatenate(
                    [zero] * ft + [p[lo:lo + n2, :TILE_KEYS], p[lo:lo + n2, TILE_KEYS:]]
                    + [zero] * (N_KEY_TILES - 2 - ft), axis=1))
            p = jnp.concatenate(full, axis=0)
            vaug = jnp.concatenate([window(v_ref, bi, k_start), ones], axis=1)
            o = jnp.dot(p, vaug, preferred_element_type=F32)
            o = o[:, :LANES] * (1.0 / o[:, LANES:])
            out = jnp.concatenate([jnp.where(head0, o[lo:lo + n], o[lo + n:lo + 2 * n])
                                   for lo, _, n, _ in slabs], axis=0)
            o_ref[bi, 0, pl.ds(pl.multiple_of(r * GRID_W, GRID_W), GRID_W), :] = out.astype(BF16)

    probabilities(0, pa_ref)

    def body(it, carry):
        g = 2 * it + 1
        probabilities(g, pb_ref)
        weighted_values(g - 1, pa_ref)
        probabilities(g + 1, pa_ref)
        weighted_values(g, pb_ref)
        return carry

    lax.fori_loop(0, n_groups // 2 - 1, body, 0)
    probabilities(n_groups - 1, pb_ref)
    weighted_values(n_groups - 2, pa_ref)
    weighted_values(n_groups - 1, pb_ref)


def _bias_selectors():
    one_hot = np.zeros((2 * WIN_COLS - 1, GRID_W, 2, KEY_TILE_COLS), np.float32)
    mask = np.full((GRID_W, 2, KEY_TILE_COLS), NEG_INF, np.float32)
    for q_lo, n, ft in Q_SLABS:
        for q in range(q_lo, q_lo + n):
            ws = min(max(q - WIN_COLS // 2, 0), GRID_W - WIN_COLS)
            for tile in range(2):
                for c16 in range(KEY_TILE_COLS):
                    c = (ft + tile) * KEY_TILE_COLS + c16
                    if ws <= c < ws + WIN_COLS:
                        one_hot[min(max(c - q + WIN_COLS - 1, 0), 2 * WIN_COLS - 2), q, tile, c16] = 1.0
                        mask[q, tile, c16] = 0.0
    return one_hot, mask


def _attn_bias(rpb):
    n_layers = rpb.shape[0]
    n_dr = 2 * WIN_ROWS - 1
    one_hot, mask = _bias_selectors()
    rpb = jnp.pad(rpb.astype(F32), ((0, 0), (0, 0), (0, REL_LANES // KEY_TILE_COLS - n_dr), (0, 0)))
    tbl = jnp.einsum("lhdk,kqtc->lhqtdc", rpb, jnp.asarray(one_hot), precision=lax.Precision.HIGHEST)
    tbl = tbl + jnp.asarray(mask)[None, None, :, :, None, :]
    tbl = tbl.reshape(n_layers, N_HEAD_BLOCKS, HEADS_PER_BLOCK, GRID_W, 2 * REL_LANES)
    slabs = [tbl[:, :, :, q_lo:q_lo + n].reshape(n_layers, N_HEAD_BLOCKS, HEADS_PER_BLOCK * n, 2 * REL_LANES)
             for q_lo, n, _ in Q_SLABS]
    return jnp.concatenate(slabs, axis=2)


def _attn_layer(q, kv, bias, mixer):
    b, _, seq2, _ = q.shape
    seq = seq2 // HEADS_PER_BLOCK
    rows = seq // GRID_W
    nb = min(ATTN_TOKENS_PER_STEP // seq, b)
    assert b % nb == 0
    kv_blk = (nb, 1, N_KEY_TILES, seq // N_KEY_TILES, LANES)
    p_scratch = pltpu.VMEM((ATTN_ROWS_PER_ITER, QBLK_ROWS, SLAB_KEYS), BF16)
    bias_scratch = pltpu.VMEM((WIN_ROWS, QBLK_ROWS, SLAB_KEYS), F32)
    return pl.pallas_call(
        functools.partial(_attn_kernel, rows=rows, nb=nb),
        grid=(N_HEAD_BLOCKS, b // nb),
        scratch_shapes=[bias_scratch, p_scratch, p_scratch],
        in_specs=[
            pl.BlockSpec((nb, 1, seq2, LANES), lambda hb, i: (i, hb, 0, 0)),
            pl.BlockSpec(kv_blk, lambda hb, i: (i, hb, 0, 0, 0)),
            pl.BlockSpec(kv_blk, lambda hb, i: (i, N_HEAD_BLOCKS + hb, 0, 0, 0)),
            pl.BlockSpec((None, 1, QBLK_ROWS, 2 * REL_LANES), lambda hb, i: (mixer, hb, 0, 0)),
        ],
        out_specs=pl.BlockSpec((nb, 1, seq, LANES), lambda hb, i: (i, hb, 0, 0)),
        out_shape=jax.ShapeDtypeStruct((b, N_HEAD_BLOCKS, seq, LANES), BF16),
        compiler_params=_params(2),
        name="nbr_attention",
    )(q, kv, kv, bias)


def _mlp(x1, gm_ref, wup_ref, wdn_ref, gf_ref, between_chunks=None):
    h = _rms(x1, gm_ref[...]).astype(BF16)
    acc = x1
    for c in range(N_FF_CHUNKS):
        ff = slice(c * FF_CHUNK, (c + 1) * FF_CHUNK)
        up = jnp.dot(h, wup_ref[:, ff], preferred_element_type=F32)
        act = jnp.square(jnp.maximum(up, 0.0)).astype(BF16)
        acc = acc + jnp.dot(act, wdn_ref[ff, :], preferred_element_type=F32)
        if between_chunks is not None:
            between_chunks(c)
    return acc if gf_ref is None else _rms(acc, gf_ref[...])


def _attn_mlp_kernel(x_ref, a_ref, wo_ref, gm_ref, wup_ref, wdn_ref, *rest, sub):
    *gf_ref, o_ref = rest
    subs = [slice(k * sub, (k + 1) * sub) for k in range(x_ref.shape[1] // sub)]
    x1 = []
    for rows in subs:
        attn = jnp.concatenate([a_ref[0, j, rows] for j in range(N_HEAD_BLOCKS)], axis=-1)
        x1.append(x_ref[0, rows] + jnp.dot(attn, wo_ref[...], preferred_element_type=F32))
    for rows, x1_rows in zip(subs, x1):
        o_ref[0, rows] = _mlp(x1_rows, gm_ref, wup_ref, wdn_ref, gf_ref[0] if gf_ref else None)


def _pool_mlp_kernel(x_ref, xp_ref, xn_ref, gx_ref, pw_ref, psc_ref, gm_ref, wup_ref, wdn_ref, *rest, sub, seq):
    *gf_ref, o_ref, ext_ref = rest
    assert N_FF_CHUNKS == N_POOL_GROUPS
    tm = x_ref.shape[1]
    s = pl.program_id(1)
    g = gx_ref[...]
    ext_ref[0:POOL_HALO, :] = jnp.where(s > 0, _rms(xp_ref[0], g), 0.0)
    ext_ref[POOL_HALO + tm:, :] = jnp.where(s < pl.num_programs(1) - 1, _rms(xn_ref[0], g), 0.0)
    subs = [slice(k * sub, (k + 1) * sub) for k in range(tm // sub)]
    for rows in subs:
        ext_ref[POOL_HALO + rows.start:POOL_HALO + rows.stop, :] = _rms(x_ref[0, rows], g)

    def pool_group(rows, gi):
        return _pool_group(ext_ref, x_ref[0, rows], rows.start, s * tm + rows.start, gi, pw_ref, psc_ref, seq)

    groups = [pool_group(subs[0], gi) for gi in range(N_POOL_GROUPS)]
    for k, rows in enumerate(subs):
        x1 = jnp.concatenate(groups, axis=1)
        groups = []
        mix_next = None
        if k + 1 < len(subs):
            mix_next = lambda gi, nxt=subs[k + 1]: groups.append(pool_group(nxt, gi))
        o_ref[0, rows] = _mlp(x1, gm_ref, wup_ref, wdn_ref, gf_ref[0] if gf_ref else None, mix_next)


def _mlp_weight_specs(d, layer, gf):
    specs = [_const_spec((1, d)), _const_spec((d, D_FF), layer), _const_spec((D_FF, d), layer)]
    return specs + ([_const_spec((1, d))] if gf is not None else [])


def _attn_mlp_layer(x, attn, wo, mixer, gm, wup, wdn, layer, gf=None):
    b, seq, d = x.shape
    tm = STEP_TILE
    args = [x, attn, wo, gm, wup, wdn] + ([gf] if gf is not None else [])
    return pl.pallas_call(
        functools.partial(_attn_mlp_kernel, sub=TOKEN_TILE),
        grid=(b, seq // tm),
        in_specs=[
            pl.BlockSpec((1, tm, d), lambda i, j: (i, j, 0)),
            pl.BlockSpec((1, N_HEAD_BLOCKS, tm, LANES), lambda i, j: (i, 0, j, 0)),
            _const_spec((d, d), mixer),
        ] + _mlp_weight_specs(d, layer, gf),
        out_specs=pl.BlockSpec((1, tm, d), lambda i, j: (i, j, 0)),
        out_shape=jax.ShapeDtypeStruct(x.shape, F32),
        compiler_params=_params(2),
        name="attn_mlp_block",
    )(*args)


def _pool_mlp_layer(x, gx, pw, mixer, psc, gm, wup, wdn, layer, gf=None):
    b, seq, d = x.shape
    tm = STEP_TILE
    halo_blocks = tm // POOL_HALO
    n_halo = seq // POOL_HALO
    args = [x, x, x, gx, pw, psc, gm, wup, wdn] + ([gf] if gf is not None else [])
    return pl.pallas_call(
        functools.partial(_pool_mlp_kernel, sub=POOL_SUB_TILE, seq=seq),
        grid=(b, seq // tm),
        in_specs=[
            pl.BlockSpec((1, tm, d), lambda i, j: (i, j, 0)),
            pl.BlockSpec((1, POOL_HALO, d),
                         lambda i, j: (i, jnp.maximum(j * halo_blocks - 1, 0), 0)),
            pl.BlockSpec((1, POOL_HALO, d),
                         lambda i, j: (i, jnp.minimum((j + 1) * halo_blocks, n_halo - 1), 0)),
            _const_spec((1, d)),
            _const_spec((N_POOL_GROUPS, POOL_GROUP, POOL_GROUP), mixer),
            _const_spec((1, d)),
        ] + _mlp_weight_specs(d, layer, gf),
        out_specs=pl.BlockSpec((1, tm, d), lambda i, j: (i, j, 0)),
        out_shape=jax.ShapeDtypeStruct(x.shape, F32),
        scratch_shapes=[pltpu.VMEM((tm + 2 * POOL_HALO, d), F32)],
        compiler_params=_params(2),
        name="pool_mlp_block",
    )(*args)


def _trunk(x, p):
    depth = p["norm_mix"].shape[0]
    for i in range(depth):
        j = i // 2
        gx = p["norm_mix"][i][None]
        mlp = (p["norm_mlp"][i][None], p["w_up"], p["w_down"], i)
        gf = p["norm_final"][None] if i == depth - 1 else None
        if i % 2 == 0:
            x = _pool_mlp_layer(x, gx, p["pool_w"], j, p["pool_scale"][j][None], *mlp, gf=gf)
        else:
            q, kv = _qkv_layer(x, gx, p["w_qkv"], j)
            attn = _attn_layer(q, kv, p["attn_bias"], j)
            x = _attn_mlp_layer(x, attn, p["w_o"], j, *mlp, gf=gf)
    return x


def kernel(x_prompt, x_sample, norm_mix, pool_w, pool_scale, w_qkv, rpb, w_o, norm_mlp, w_up, w_down, norm_final):
    p = {
        "norm_mix": norm_mix.astype(F32),
        "norm_mlp": norm_mlp.astype(F32),
        "norm_final": norm_final.astype(F32),
        "pool_scale": pool_scale.astype(F32),
        "pool_w": pool_w.astype(BF16),
        "w_qkv": w_qkv.astype(BF16),
        "w_o": w_o.astype(BF16),
        "w_up": w_up.astype(BF16),
        "w_down": w_down.astype(BF16),
        "attn_bias": _attn_bias(rpb),
    }
    return (_trunk(x_prompt, p), _trunk(x_sample, p))
```
